```python
import functools
import jax
import jax.numpy as jnp
from jax import lax
import numpy as np

D_MODEL = 1024
BATCH = 8
SEQ = 2048
DEPTH = 1
DEC_BATCH = 32
DEC_SEQ = 4
PAST_LEN = 8192
PAGE_SIZE = 128

HEAD_DIM = 64
SB_HEADS = D_MODEL // (2 * HEAD_DIM)
MB_HEADS = D_MODEL // (2 * HEAD_DIM)
SB_WIDTH = SB_HEADS * HEAD_DIM
MB_WIDTH = MB_HEADS * HEAD_DIM
Q_BLOCK = 128
MOBA_BLOCK = 256
MOBA_TOPK = 3
ROPE_THETA = 10000.0
N_EXPERTS = 32
TOP_K = 4
D_FF = D_MODEL
SWIGLU_LIMIT = 7.0
SWIGLU_ALPHA = 1.702
ROUTE_BLOCK = 128
NORM_EPS = 1e-6
N_MOD = 6
ATTN_SCALE = HEAD_DIM ** -0.5
IN_SPLITS = (SB_WIDTH, SB_WIDTH, SB_WIDTH, MB_WIDTH, MB_WIDTH, MB_WIDTH, D_MODEL, D_MODEL)
IN_COLS = sum(IN_SPLITS)

kernel_name = 'hybrid_stickbreak_moba_moe_step'


def rms_norm(x, g):
    xf = x.astype(jnp.float32)
    y = xf * lax.rsqrt(jnp.mean(xf * xf, axis=-1, keepdims=True) + NORM_EPS)
    return (y * g.astype(jnp.float32)).astype(x.dtype)


def rope(x, pos):
    inv_freq = ROPE_THETA ** (-jnp.arange(0, HEAD_DIM, 2, dtype=jnp.float32) / HEAD_DIM)
    ang = pos.astype(jnp.float32)[:, None] * inv_freq[None, :]
    cos = jnp.cos(ang)[None, :, None, :]
    sin = jnp.sin(ang)[None, :, None, :]
    x1, x2 = jnp.split(x.astype(jnp.float32), 2, axis=-1)
    return jnp.concatenate([x1 * cos - x2 * sin, x2 * cos + x1 * sin], axis=-1).astype(x.dtype)


def adaln(c, w_ada, b_ada):
    mod = jax.nn.silu(c) @ w_ada + b_ada
    return jnp.split(mod[:, None, :], N_MOD, axis=-1)


def modulate(x, g, shift, scale):
    return rms_norm(x, g) * (1.0 + scale) + shift


def project_mixer_inputs(h, pos, w_in, q_norm_g, k_norm_g):
    b, t, _ = h.shape
    p = h @ w_in
    bounds = np.cumsum(IN_SPLITS)[:-1].tolist()
    q_sb, k_sb, v_sb, q_mb, k_mb, v_mb, g_sb, g_mb = jnp.split(p, bounds, axis=-1)
    heads = lambda z: z.reshape(b, t, -1, HEAD_DIM)
    q_mb = rope(rms_norm(heads(q_mb), q_norm_g), pos)
    k_mb = rope(rms_norm(heads(k_mb), k_norm_g), pos)
    return heads(q_sb), heads(k_sb), heads(v_sb), q_mb, k_mb, heads(v_mb), g_sb, g_mb


def stick_breaking_weights(z, allowed):
    log_keep = jnp.where(allowed, jax.nn.log_sigmoid(-z), 0.0)
    later = lax.cumsum(log_keep, axis=z.ndim - 1, reverse=True) - log_keep
    return jnp.where(allowed, jnp.exp(jax.nn.log_sigmoid(z) + later), 0.0)


def sb_prompt(q, k, v):
    b, s, h, d = q.shape
    n_qb = s // Q_BLOCK
    k_pos = jnp.arange(s)
    q_blocks = q.reshape(b, n_qb, Q_BLOCK, h, d).swapaxes(0, 1)

    def one_block(args):
        q_blk, start = args
        q_pos = start + jnp.arange(Q_BLOCK)
        z = jnp.einsum('bthd,bshd->bhts', q_blk, k).astype(jnp.float32) * ATTN_SCALE
        a = stick_breaking_weights(z, k_pos[None, :] < q_pos[:, None])
        return jnp.einsum('bhts,bshd->bthd', a.astype(v.dtype), v)

    out = lax.map(one_block, (q_blocks, jnp.arange(n_qb) * Q_BLOCK))
    return out.swapaxes(0, 1).reshape(b, s, h * d)


def sb_sample(q, k_past, v_past, k_new, v_new):
    b, t, h, d = q.shape
    past = k_past.shape[1]
    q_pos = past + jnp.arange(t)
    k_pos = jnp.arange(past + t)
    z = jnp.concatenate([jnp.einsum('bthd,bshd->bhts', q, k_past),
                         jnp.einsum('bthd,bshd->bhts', q, k_new)], axis=-1)
    z = z.astype(jnp.float32) * ATTN_SCALE
    a = stick_breaking_weights(z, k_pos[None, :] < q_pos[:, None]).astype(v_past.dtype)
    out = (jnp.einsum('bhts,bshd->bthd', a[..., :past], v_past)
           + jnp.einsum('bhts,bshd->bthd', a[..., past:], v_new))
    return out.reshape(b, t, h * d)


def pad_blocks(x):
    pad = (-x.shape[1]) % MOBA_BLOCK
    return jnp.pad(x, ((0, 0), (0, pad), (0, 0), (0, 0)))


def block_means(k):
    b, l, h, d = k.shape
    kb = k.reshape(b, l // MOBA_BLOCK, MOBA_BLOCK, h, d).astype(jnp.float32)
    return jnp.mean(kb, axis=2).astype(k.dtype)


def moba_group(q, q_pos, k, v, k_mean):
    n_kb, h, _ = k_mean.shape
    t = q.shape[0]
    k_sel = min(MOBA_TOPK, n_kb)
    own = q_pos[0] // MOBA_BLOCK
    gate = jnp.einsum('thd,nhd->htn', q, k_mean).astype(jnp.float32)
    gate = jnp.where(jnp.arange(n_kb) < own, gate, -jnp.inf)
    _, sel = lax.top_k(gate, k_sel)
    sel_ok = jnp.arange(k_sel) < own
    k_blocks = k.reshape(n_kb, MOBA_BLOCK, h, HEAD_DIM).transpose(2, 0, 1, 3)
    v_blocks = v.reshape(n_kb, MOBA_BLOCK, h, HEAD_DIM).transpose(2, 0, 1, 3)
    h_idx = jnp.arange(h)[:, None, None]
    k_g = k_blocks[h_idx, sel]
    v_g = v_blocks[h_idx, sel]
    s_sel = jnp.einsum('thd,htkbd->htkb', q, k_g).astype(jnp.float32) * ATTN_SCALE
    s_sel = jnp.where(sel_ok[:, None], s_sel, -jnp.inf).reshape(h, t, k_sel * MOBA_BLOCK)
    own_start = own * MOBA_BLOCK
    own_k = lax.dynamic_slice_in_dim(k, own_start, MOBA_BLOCK, axis=0)
    own_v = lax.dynamic_slice_in_dim(v, own_start, MOBA_BLOCK, axis=0)
    own_pos = own_start + jnp.arange(MOBA_BLOCK)
    s_own = jnp.einsum('thd,bhd->htb', q, own_k).astype(jnp.float32) * ATTN_SCALE
    s_own = jnp.where(own_pos[None, :] <= q_pos[:, None], s_own, -jnp.inf)
    p = jax.nn.softmax(jnp.concatenate([s_sel, s_own], axis=-1), axis=-1).astype(v.dtype)
    p_sel = p[..., :k_sel * MOBA_BLOCK].reshape(h, t, k_sel, MOBA_BLOCK)
    p_own = p[..., k_sel * MOBA_BLOCK:]
    return (jnp.einsum('htkb,htkbd->thd', p_sel, v_g)
            + jnp.einsum('htb,bhd->thd', p_own, own_v))


def moba_prompt(q, k, v):
    b, s, h, d = q.shape
    kp, vp = pad_blocks(k), pad_blocks(v)
    km = block_means(kp)
    n_qb = s // Q_BLOCK
    q_items = q.reshape(b * n_qb, Q_BLOCK, h, d)
    item = jnp.arange(b * n_qb)
    b_idx = item // n_qb
    starts = (item % n_qb) * Q_BLOCK

    def one_item(args):
        q_blk, bi, start = args
        return moba_group(q_blk, start + jnp.arange(Q_BLOCK), kp[bi], vp[bi], km[bi])

    out = lax.map(one_item, (q_items, b_idx, starts))
    return out.reshape(b, s, h * d)


def moba_sample(q, k_past, v_past, k_new, v_new):
    b, t, h, d = q.shape
    past = k_past.shape[1]
    kf = pad_blocks(jnp.concatenate([k_past, k_new], axis=1))
    vf = pad_blocks(jnp.concatenate([v_past, v_new], axis=1))
    km = block_means(kf)
    q_pos = past + jnp.arange(t)
    out = jax.vmap(moba_group, in_axes=(0, None, 0, 0, 0))(q, q_pos, kf, vf, km)
    return out.reshape(b, t, h * d)


def merge_branches(o_sb, o_mb, g_sb, g_mb, w_br_sb, w_br_mb, w_out):
    u = jax.nn.sigmoid(g_sb) * (o_sb @ w_br_sb) + jax.nn.sigmoid(g_mb) * (o_mb @ w_br_mb)
    return u @ w_out


def moe_ffn(x, w_router, b_router, w_gu, b_gu, w_down, b_down):
    n, d = x.shape
    logits = (x @ w_router).astype(jnp.float32) + b_router.astype(jnp.float32)
    top_logit, top_e = lax.top_k(logits, TOP_K)
    gate = jax.nn.softmax(top_logit, axis=-1).astype(x.dtype)
    nk = n * TOP_K
    flat_e = top_e.reshape(-1)
    order = jnp.argsort(flat_e)
    e_sorted = flat_e[order]
    tok_sorted = order // TOP_K
    counts = jnp.bincount(flat_e, length=N_EXPERTS)
    padded = (counts + ROUTE_BLOCK - 1) // ROUTE_BLOCK * ROUTE_BLOCK
    start = jnp.cumsum(counts) - counts
    pend = jnp.cumsum(padded)
    pstart = pend - padded
    dest = pstart[e_sorted] + jnp.arange(nk) - start[e_sorted]
    n_blocks = (nk + N_EXPERTS * (ROUTE_BLOCK - 1) + ROUTE_BLOCK - 1) // ROUTE_BLOCK
    rows = n_blocks * ROUTE_BLOCK
    row_tok = jnp.full((rows,), n, jnp.int32).at[dest].set(tok_sorted.astype(jnp.int32))
    row_gate = jnp.zeros((rows,), x.dtype).at[dest].set(gate.reshape(-1)[order])
    block_e = jnp.minimum(
        jnp.searchsorted(pend, jnp.arange(n_blocks) * ROUTE_BLOCK, side='right'), N_EXPERTS - 1)
    x_rows = jnp.concatenate([x, jnp.zeros((1, d), x.dtype)], axis=0)[row_tok]
    x_rows = x_rows.reshape(n_blocks, ROUTE_BLOCK, d)

    def expert_block(args):
        xb, e = args
        gu = xb @ w_gu[e] + b_gu[e]
        g, u = jnp.split(gu, 2, axis=-1)
        g = jnp.minimum(g, SWIGLU_LIMIT)
        u = jnp.clip(u, -SWIGLU_LIMIT, SWIGLU_LIMIT)
        glu = g * jax.nn.sigmoid(SWIGLU_ALPHA * g)
        return ((u + 1.0) * glu) @ w_down[e] + b_down[e]

    y_rows = lax.map(expert_block, (x_rows, block_e)).reshape(rows, d)
    y = jax.ops.segment_sum(y_rows * row_gate[:, None], row_tok, num_segments=n + 1)
    return y[:n]


def decoder_layer(x, c, pos, attend, w_ada, b_ada, g_norm1, g_norm2, w_in, q_norm_g,
                  k_norm_g, w_br_sb, w_br_mb, w_out, w_router, b_router, w_gu, b_gu,
                  w_down, b_down):
    sh1, sc1, ga1, sh2, sc2, ga2 = adaln(c, w_ada, b_ada)
    h = modulate(x, g_norm1, sh1, sc1)
    q_sb, k_sb, v_sb, q_mb, k_mb, v_mb, g_sb, g_mb = project_mixer_inputs(
        h, pos, w_in, q_norm_g, k_norm_g)
    o_sb, o_mb = attend(q_sb, k_sb, v_sb, q_mb, k_mb, v_mb)
    x = x + ga1 * merge_branches(o_sb, o_mb, g_sb, g_mb, w_br_sb, w_br_mb, w_out)
    h2 = modulate(x, g_norm2, sh2, sc2)
    b, t, dm = h2.shape
    ffn = moe_ffn(h2.reshape(b * t, dm), w_router, b_router, w_gu, b_gu, w_down, b_down)
    x = x + ga2 * ffn.reshape(b, t, dm)
    return x, (k_sb, v_sb, k_mb, v_mb)


def prompt_attend(q_sb, k_sb, v_sb, q_mb, k_mb, v_mb):
    return sb_prompt(q_sb, k_sb, v_sb), moba_prompt(q_mb, k_mb, v_mb)


def sample_attend(past_rows, q_sb, k_sb, v_sb, q_mb, k_mb, v_mb):
    sb_k_past, sb_v_past, mb_k_past, mb_v_past = past_rows
    return (sb_sample(q_sb, sb_k_past, sb_v_past, k_sb, v_sb),
            moba_sample(q_mb, mb_k_past, mb_v_past, k_mb, v_mb))


def gather_pages(cache, layer, page_table):
    rows = cache[layer, page_table]
    db, n_p, ps, h, d = rows.shape
    return rows.reshape(db, n_p * ps, h, d)


def setup_inputs(seed: int = 0) -> dict:
    key = jax.random.key(seed)
    ks = jax.random.split(key, 26)
    f32 = jnp.float32

    def nrm(k, shape, scale=1.0):
        return jax.random.normal(k, shape, f32) * scale

    n_pages = PAST_LEN // PAGE_SIZE
    n_used = DEC_BATCH * n_pages
    n_phys = (5 * n_used + 3) // 4
    page_table = jax.random.permutation(ks[8], n_phys)[:n_used].reshape(
        DEC_BATCH, n_pages).astype(jnp.int32)
    sb_cache = (DEPTH, n_phys, PAGE_SIZE, SB_HEADS, HEAD_DIM)
    mb_cache = (DEPTH, n_phys, PAGE_SIZE, MB_HEADS, HEAD_DIM)
    L = DEPTH
    return {
        'x_prompt': nrm(ks[0], (BATCH, SEQ, D_MODEL)),
        'x_sample': nrm(ks[1], (DEC_BATCH, DEC_SEQ, D_MODEL)),
        'c_prompt': nrm(ks[2], (BATCH, D_MODEL)),
        'c_sample': nrm(ks[3], (DEC_BATCH, D_MODEL)),
        'cache_sb_k': nrm(ks[4], sb_cache),
        'cache_sb_v': nrm(ks[5], sb_cache),
        'cache_mb_k': nrm(ks[6], mb_cache),
        'cache_mb_v': nrm(ks[7], mb_cache),
        'page_table': page_table,
        'w_ada': nrm(ks[9], (L, D_MODEL, N_MOD * D_MODEL), D_MODEL ** -0.5),
        'b_ada': nrm(ks[10], (L, N_MOD * D_MODEL), 0.02),
        'g_norm1': 1.0 + nrm(ks[11], (L, D_MODEL), 0.02),
        'g_norm2': 1.0 + nrm(ks[12], (L, D_MODEL), 0.02),
        'w_in': nrm(ks[13], (L, D_MODEL, IN_COLS), D_MODEL ** -0.5),
        'q_norm_g': 1.0 + nrm(ks[14], (L, HEAD_DIM), 0.02),
        'k_norm_g': 1.0 + nrm(ks[15], (L, HEAD_DIM), 0.02),
        'w_br_sb': nrm(ks[16], (L, SB_WIDTH, D_MODEL), SB_WIDTH ** -0.5),
        'w_br_mb': nrm(ks[17], (L, MB_WIDTH, D_MODEL), MB_WIDTH ** -0.5),
        'w_out': nrm(ks[18], (L, D_MODEL, D_MODEL), D_MODEL ** -0.5),
        'w_router': nrm(ks[19], (L, D_MODEL, N_EXPERTS), D_MODEL ** -0.5),
        'b_router': nrm(ks[20], (L, N_EXPERTS), 0.01),
        'w_gu': nrm(ks[21], (L, N_EXPERTS, D_MODEL, 2 * D_FF), D_MODEL ** -0.5),
        'b_gu': nrm(ks[22], (L, N_EXPERTS, 2 * D_FF), 0.02),
        'w_down': nrm(ks[23], (L, N_EXPERTS, D_FF, D_MODEL), D_FF ** -0.5),
        'b_down': nrm(ks[24], (L, N_EXPERTS, D_MODEL), 0.02),
    }


def reference(x_prompt, x_sample, c_prompt, c_sample, cache_sb_k, cache_sb_v, cache_mb_k,
              cache_mb_v, page_table, w_ada, b_ada, g_norm1, g_norm2, w_in, q_norm_g,
              k_norm_g, w_br_sb, w_br_mb, w_out, w_router, b_router, w_gu, b_gu, w_down,
              b_down):
    seq = x_prompt.shape[1]
    past = page_table.shape[1] * cache_sb_k.shape[2]
    pos_prompt = jnp.arange(seq, dtype=jnp.int32)
    pos_sample = past + jnp.arange(x_sample.shape[1], dtype=jnp.int32)
    y_prompt, y_sample = x_prompt, x_sample
    rows_prompt, rows_sample = [], []
    for layer in range(DEPTH):
        lw = (w_ada[layer], b_ada[layer], g_norm1[layer], g_norm2[layer], w_in[layer],
              q_norm_g[layer], k_norm_g[layer], w_br_sb[layer], w_br_mb[layer],
              w_out[layer], w_router[layer], b_router[layer], w_gu[layer], b_gu[layer],
              w_down[layer], b_down[layer])
        y_prompt, new_p = decoder_layer(y_prompt, c_prompt, pos_prompt, prompt_attend, *lw)
        past_rows = (gather_pages(cache_sb_k, layer, page_table),
                     gather_pages(cache_sb_v, layer, page_table),
                     gather_pages(cache_mb_k, layer, page_table),
                     gather_pages(cache_mb_v, layer, page_table))
        y_sample, new_s = decoder_layer(y_sample, c_sample, pos_sample,
                                        functools.partial(sample_attend, past_rows), *lw)
        rows_prompt.append(new_p)
        rows_sample.append(new_s)
    sb_k_prompt = jnp.stack([r[0] for r in rows_prompt])
    sb_v_prompt = jnp.stack([r[1] for r in rows_prompt])
    mb_k_prompt = jnp.stack([r[2] for r in rows_prompt])
    mb_v_prompt = jnp.stack([r[3] for r in rows_prompt])
    sb_k_sample = jnp.stack([r[0] for r in rows_sample])
    sb_v_sample = jnp.stack([r[1] for r in rows_sample])
    mb_k_sample = jnp.stack([r[2] for r in rows_sample])
    mb_v_sample = jnp.stack([r[3] for r in rows_sample])
    return (y_prompt, y_sample, sb_k_prompt, sb_v_prompt, mb_k_prompt, mb_v_prompt,
            sb_k_sample, sb_v_sample, mb_k_sample, mb_v_sample)
```

```python
import functools

import numpy as np
import jax
import jax.numpy as jnp
from jax import lax
from jax.experimental import pallas as pl
from jax.experimental.pallas import tpu as pltpu

HEAD_DIM = 64
N_HEADS = 8
WIDTH = N_HEADS * HEAD_DIM
MOBA_BLOCK = 256
MOBA_TOPK = 3
ROPE_THETA = 10000.0
N_EXPERTS = 32
TOP_K = 4
SWIGLU_LIMIT = 7.0
SWIGLU_ALPHA = 1.702
NORM_EPS = 1e-6
N_MOD = 6
ATTN_SCALE = HEAD_DIM ** -0.5

LANES = 128
ROW_TILE = 256
ATTN_TILE = 256
EXPERT_TILE = 512
PAGES_PER_STEP = 8
NEG_BIG = -1e30
VMEM_LIMIT = 48 * 1024 * 1024

F32 = jnp.float32
BF16 = jnp.bfloat16


def _dot(a, b):
    return jnp.dot(a, b, preferred_element_type=F32)


def _dot_nt(a, b):
    return lax.dot_general(a, b, (((1,), (1,)), ((), ())), preferred_element_type=F32)


def _split(x):
    hi = x.astype(BF16)
    lo = (x - hi.astype(F32)).astype(BF16)
    return hi, lo


def _dot3(a, b):
    a_hi, a_lo = _split(a)
    b_hi, b_lo = _split(b)
    return _dot(a_hi, b_hi) + _dot(a_lo, b_hi) + _dot(a_hi, b_lo)


def _params(sem, vmem=VMEM_LIMIT):
    return pltpu.CompilerParams(dimension_semantics=sem, vmem_limit_bytes=vmem)


def _adaln_body(c_ref, w_ref, b_ref, o_ref):
    c = c_ref[...]
    s = c * jax.nn.sigmoid(c)
    o_ref[...] = _dot3(s, w_ref[...]) + b_ref[...]


def _adaln(c, w_ada, b_ada):
    rows, d = c.shape
    cols = w_ada.shape[1]
    tn = 1024
    return pl.pallas_call(
        _adaln_body,
        out_shape=jax.ShapeDtypeStruct((rows, cols), F32),
        grid=(cols // tn,),
        in_specs=[pl.BlockSpec((rows, d), lambda j: (0, 0)),
                  pl.BlockSpec((d, tn), lambda j: (0, j)),
                  pl.BlockSpec((1, tn), lambda j: (0, j))],
        out_specs=pl.BlockSpec((rows, tn), lambda j: (0, j)),
        compiler_params=_params(("arbitrary",)),
        name="adaln",
    )(c, w_ada, b_ada.reshape(1, cols))


def _modulate(x, g, shift, scale):
    ms = jnp.mean(x * x, axis=-1, keepdims=True)
    return (x * lax.rsqrt(ms + NORM_EPS) * g) * (1.0 + scale) + shift


def _qk_norm_rope(p, g, gmat, cos, sin_signed):
    ms = _dot((p * p).astype(BF16), gmat)
    y = p * lax.rsqrt(ms + NORM_EPS) * g
    lane = lax.broadcasted_iota(jnp.int32, (p.shape[0], LANES), 1)
    first_half = (lane & (HEAD_DIM // 2)) == 0
    outs = []
    for c in range(WIDTH // LANES):
        yc = y[:, c * LANES:(c + 1) * LANES]
        partner = jnp.where(first_half, pltpu.roll(yc, LANES - HEAD_DIM // 2, 1),
                            pltpu.roll(yc, HEAD_DIM // 2, 1))
        outs.append(yc * cos + partner * sin_signed)
    return jnp.concatenate(outs, axis=1)


def _inproj_body(emit_kmean, x_ref, sh_ref, sc_ref, g1_ref, w_ref, cos_ref, sin_ref, qg_ref, kg_ref,
                 gmat_ref, qsb_ref, ksb_ref, vsb_ref, ksb16_ref, vsb16_ref, qmb_ref, kmb_ref, vmb_ref,
                 kmb16_ref, vmb16_ref, sgsb_ref, sgmb_ref, *rest):
    h = _modulate(x_ref[0], g1_ref[...], sh_ref[0], sc_ref[0])
    h16 = h.astype(BF16)

    def proj(c0, width):
        return _dot(h16, w_ref[:, c0:c0 + width])

    w = WIDTH
    qsb_ref[0] = (proj(0, w) * ATTN_SCALE).astype(BF16)
    ksb = proj(w, w)
    ksb_ref[0] = ksb
    ksb16_ref[0] = ksb.astype(BF16)
    vsb = proj(2 * w, w)
    vsb_ref[0] = vsb
    vsb16_ref[0] = vsb.astype(BF16)
    cos = cos_ref[...]
    sin = sin_ref[...]
    gmat = gmat_ref[...]
    qmb = _qk_norm_rope(proj(3 * w, w), qg_ref[...], gmat, cos, sin)
    qmb_ref[0] = (qmb * ATTN_SCALE).astype(BF16)
    kmb = _qk_norm_rope(proj(4 * w, w), kg_ref[...], gmat, cos, sin)
    kmb_ref[0] = kmb
    kmb16_ref[0] = kmb.astype(BF16)
    vmb = proj(5 * w, w)
    vmb_ref[0] = vmb
    vmb16_ref[0] = vmb.astype(BF16)
    d = x_ref.shape[2]
    sgsb_ref[0] = jax.nn.sigmoid(proj(6 * w, d)).astype(BF16)
    sgmb_ref[0] = jax.nn.sigmoid(proj(6 * w + d, d)).astype(BF16)
    if emit_kmean:
        kmean_ref = rest[0]
        ti = pl.program_id(1)
        kmean_ref[0, pl.ds(ti, 1), :] = jnp.mean(kmb, axis=0, keepdims=True)


def _rope_tables(pos):
    inv_freq = ROPE_THETA ** (-jnp.arange(0, HEAD_DIM, 2, dtype=F32) / HEAD_DIM)
    ang = pos.astype(F32)[:, None] * inv_freq[None, :]
    cos = jnp.tile(jnp.cos(ang), (1, LANES // (HEAD_DIM // 2)))
    sin = jnp.sin(ang)
    sin_signed = jnp.tile(jnp.concatenate([-sin, sin], axis=1), (1, LANES // HEAD_DIM))
    return cos, sin_signed


def _inproj(x, shift, scale, g1, w_in16, pos, qg, kg, gmat, tm, emit_kmean):
    b, t, d = x.shape
    r = shift.shape[1]
    cols = w_in16.shape[1]
    nt = t // tm
    cos, sin_signed = _rope_tables(pos)
    row = lambda dt, wd: jax.ShapeDtypeStruct((b, t, wd), dt)
    tile = lambda wd: pl.BlockSpec((1, tm, wd), lambda bi, ti: (bi, ti, 0))
    const = lambda shp: pl.BlockSpec(shp, lambda bi, ti: (0,) * len(shp))
    mod = pl.BlockSpec((1, r, d), lambda bi, ti: (bi, 0, 0))
    out_shape = [row(BF16, WIDTH), row(F32, WIDTH), row(F32, WIDTH), row(BF16, WIDTH), row(BF16, WIDTH),
                 row(BF16, WIDTH), row(F32, WIDTH), row(F32, WIDTH), row(BF16, WIDTH), row(BF16, WIDTH),
                 row(BF16, d), row(BF16, d)]
    out_specs = [tile(WIDTH)] * 10 + [tile(d)] * 2
    if emit_kmean:
        assert tm == MOBA_BLOCK
        out_shape.append(jax.ShapeDtypeStruct((b, nt, WIDTH), F32))
        out_specs.append(pl.BlockSpec((1, nt, WIDTH), lambda bi, ti: (bi, 0, 0)))
    return pl.pallas_call(
        functools.partial(_inproj_body, emit_kmean),
        out_shape=out_shape,
        grid=(b, nt),
        in_specs=[tile(d), mod, mod, const((1, d)), const((d, cols)),
                  pl.BlockSpec((tm, LANES), lambda bi, ti: (ti, 0)),
                  pl.BlockSpec((tm, LANES), lambda bi, ti: (ti, 0)),
                  const((1, WIDTH)), const((1, WIDTH)), const((WIDTH, WIDTH))],
        out_specs=out_specs,
        compiler_params=_params(("arbitrary", "arbitrary")),
        name="inproj_kmean" if emit_kmean else "inproj",
    )(x, shift, scale, g1, w_in16, cos, sin_signed, qg, kg, gmat)


def _stick_tile(z, allowed, tri, carry):
    sp = jnp.maximum(z, 0.0) + jnp.log(1.0 + jnp.exp(-jnp.abs(z)))
    log_keep = -sp
    if allowed is not None:
        log_keep = jnp.where(allowed, log_keep, 0.0)
    lk_hi, lk_lo = _split(log_keep)
    later = _dot(lk_hi, tri) + _dot(lk_lo, tri)
    ones = jnp.ones((z.shape[1], LANES), BF16)
    total = _dot(lk_hi, ones) + _dot(lk_lo, ones)
    reps = z.shape[1] // LANES
    carry_wide = carry if reps == 1 else jnp.concatenate([carry] * reps, axis=1)
    a = jnp.exp((z - sp) + later + carry_wide)
    if allowed is not None:
        a = jnp.where(allowed, a, 0.0)
    return a, carry + total


def _head_masks(rows):
    lane = lax.broadcasted_iota(jnp.int32, (rows, LANES), 1)
    return lane < HEAD_DIM


def _sb_prompt_body(q_ref, k_ref, v_ref, tri_ref, o_ref, *, tile):
    qi = pl.program_id(2)
    q = q_ref[0]
    tri = tri_ref[...]
    low = _head_masks(tile)
    row = lax.broadcasted_iota(jnp.int32, (tile, tile), 0)
    col = lax.broadcasted_iota(jnp.int32, (tile, tile), 1)
    diag_allowed = col < row
    accs = []
    for h in range(2):
        qm = jnp.where(low if h == 0 else jnp.logical_not(low), q, jnp.zeros_like(q))

        def key_tile(kb, carry, acc, allowed):
            start = pl.multiple_of(kb * tile, tile)
            k_blk = k_ref[0, pl.ds(start, tile), :]
            v_blk = v_ref[0, pl.ds(start, tile), :]
            a, carry = _stick_tile(_dot_nt(qm, k_blk), allowed, tri, carry)
            return carry, acc + _dot(a.astype(BF16), v_blk)

        zero = jnp.zeros((tile, LANES), F32)
        carry, acc = key_tile(qi, zero, zero, diag_allowed)

        def body(i, c):
            return key_tile(qi - 1 - i, c[0], c[1], None)

        carry, acc = lax.fori_loop(0, qi, body, (carry, acc))
        accs.append(acc)
    o_ref[0] = jnp.where(low, accs[0], accs[1]).astype(o_ref.dtype)


def _tri(n):
    j = np.arange(n)[:, None]
    s = np.arange(n)[None, :]
    return jnp.asarray((j > s).astype(np.float32), dtype=BF16)


def _sb_prompt(q16, k16, v16):
    b, s, w = q16.shape
    tile = ATTN_TILE
    pairs = w // LANES
    return pl.pallas_call(
        functools.partial(_sb_prompt_body, tile=tile),
        out_shape=jax.ShapeDtypeStruct((b, s, w), BF16),
        grid=(b, pairs, s // tile),
        in_specs=[pl.BlockSpec((1, tile, LANES), lambda bi, p, qi: (bi, qi, p)),
                  pl.BlockSpec((1, s, LANES), lambda bi, p, qi: (bi, 0, p)),
                  pl.BlockSpec((1, s, LANES), lambda bi, p, qi: (bi, 0, p)),
                  pl.BlockSpec((tile, tile), lambda bi, p, qi: (0, 0))],
        out_specs=pl.BlockSpec((1, tile, LANES), lambda bi, p, qi: (bi, qi, p)),
        compiler_params=_params(("arbitrary",) * 3),
        name="sb_prompt",
    )(q16, k16, v16, _tri(tile))


def _moba_prompt_body(q_ref, k_ref, v_ref, km_ref, o_ref, sel_ref, *, tile, nb):
    own = pl.program_id(2)
    q = q_ref[0]
    km = km_ref[0]
    km_pad = jnp.concatenate([km, jnp.zeros((LANES - nb, LANES), F32)], axis=0)
    km_hi, km_lo = _split(km_pad)
    low = _head_masks(tile)
    lane = lax.broadcasted_iota(jnp.int32, (tile, LANES), 1)
    row = lax.broadcasted_iota(jnp.int32, (tile, tile), 0)
    col = lax.broadcasted_iota(jnp.int32, (tile, tile), 1)
    causal = col <= row
    past = lane < own
    accs = []
    for h in range(2):
        qm = jnp.where(low if h == 0 else jnp.logical_not(low), q, jnp.zeros_like(q))
        gate = _dot_nt(qm, km_hi) + _dot_nt(qm, km_lo)
        gate = jnp.where(past, gate, -jnp.inf)
        for n in range(nb):
            g_n = gate[:, n:n + 1]
            ahead = jnp.where(gate > g_n, 1.0, jnp.where((gate == g_n) & (lane < n), 1.0, 0.0))
            rank = jnp.sum(jnp.where(past, ahead, 0.0), axis=1, keepdims=True)
            chosen = jnp.where(rank < MOBA_TOPK, 1.0, 0.0)
            sel_ref[h, n] = jnp.broadcast_to(chosen, (tile, LANES))

        def key_tile(kb, m, l, acc, allowed):
            start = pl.multiple_of(kb * tile, tile)
            k_blk = k_ref[0, pl.ds(start, tile), :]
            v_blk = v_ref[0, pl.ds(start, tile), :]
            s = jnp.where(allowed, _dot_nt(qm, k_blk), NEG_BIG)
            m_new = jnp.maximum(m, jnp.max(s, axis=1, keepdims=True))
            p = jnp.where(allowed, jnp.exp(s - m_new), 0.0)
            alpha = jnp.exp(m - m_new)
            l = alpha * l + jnp.sum(p, axis=1, keepdims=True)
            acc = alpha * acc + _dot(p.astype(BF16), v_blk)
            return m_new, l, acc

        m0 = jnp.full((tile, 1), NEG_BIG, F32)
        l0 = jnp.zeros((tile, 1), F32)
        acc0 = jnp.zeros((tile, LANES), F32)
        m, l, acc = key_tile(own, m0, l0, acc0, causal)

        def body(kb, c):
            chosen = sel_ref[h, kb]
            allowed = jnp.concatenate([chosen] * (tile // LANES), axis=1) > 0.5
            return key_tile(kb, c[0], c[1], c[2], allowed)

        m, l, acc = lax.fori_loop(0, own, body, (m, l, acc))
        accs.append(acc / l)
    o_ref[0] = jnp.where(low, accs[0], accs[1]).astype(o_ref.dtype)


def _moba_prompt(q16, k16, v16, kmean):
    b, s, w = q16.shape
    tile = MOBA_BLOCK
    nb = s // tile
    pairs = w // LANES
    return pl.pallas_call(
        functools.partial(_moba_prompt_body, tile=tile, nb=nb),
        out_shape=jax.ShapeDtypeStruct((b, s, w), BF16),
        grid=(b, pairs, nb),
        in_specs=[pl.BlockSpec((1, tile, LANES), lambda bi, p, qi: (bi, qi, p)),
                  pl.BlockSpec((1, s, LANES), lambda bi, p, qi: (bi, 0, p)),
                  pl.BlockSpec((1, s, LANES), lambda bi, p, qi: (bi, 0, p)),
                  pl.BlockSpec((1, nb, LANES), lambda bi, p, qi: (bi, 0, p))],
        out_specs=pl.BlockSpec((1, tile, LANES), lambda bi, p, qi: (bi, qi, p)),
        scratch_shapes=[pltpu.VMEM((2, nb, tile, LANES), F32)],
        compiler_params=_params(("arbitrary",) * 3),
        name="moba_prompt",
    )(q16, k16, v16, kmean)


def _page_specs(n_pages, page, width, newest_first):
    specs = []
    for i in range(PAGES_PER_STEP):
        def index(bi, s, pt, i=i):
            logical = s * PAGES_PER_STEP + i
            return (pt[bi, n_pages - 1 - logical if newest_first else logical], 0, 0)
        specs.append(pl.BlockSpec((None, page, width), index))
    return specs


def _sb_sample_body(pt_ref, q_ref, kn_ref, vn_ref, tri_ref, hm_ref, *rest, page, t_new):
    k_refs = rest[:PAGES_PER_STEP]
    v_refs = rest[PAGES_PER_STEP:2 * PAGES_PER_STEP]
    o_ref = rest[2 * PAGES_PER_STEP]
    carry_ref, acc_ref = rest[2 * PAGES_PER_STEP + 1:]
    step = pl.program_id(1)
    q = q_ref[0]
    rows = q.shape[0]
    tri = tri_ref[...]

    @pl.when(step == 0)
    def _():
        kn = kn_ref[0].astype(BF16)
        vn = vn_ref[0].astype(BF16)
        key = lax.broadcasted_iota(jnp.int32, (rows, LANES), 1)
        qt = lax.broadcasted_iota(jnp.int32, (rows, LANES), 0) % t_new
        a, carry = _stick_tile(_dot_nt(q, kn), key < qt, tri, jnp.zeros((rows, LANES), F32))
        carry_ref[...] = carry
        acc_ref[...] = _dot(a.astype(BF16), vn)

    carry = carry_ref[...]
    acc = acc_ref[...]
    for i in range(PAGES_PER_STEP):
        k_pg = k_refs[i][...].astype(BF16)
        v_pg = v_refs[i][...].astype(BF16)
        a, carry = _stick_tile(_dot_nt(q, k_pg), None, tri, carry)
        acc = acc + _dot(a.astype(BF16), v_pg)
    carry_ref[...] = carry
    acc_ref[...] = acc

    @pl.when(step == pl.num_programs(1) - 1)
    def _():
        o_ref[0] = acc * hm_ref[...]


def _expand_heads(q):
    b, t, w = q.shape
    head_of_lane = jnp.arange(w) // HEAD_DIM
    mask = (head_of_lane[None, :] == jnp.arange(N_HEADS)[:, None]).astype(q.dtype)
    return (q[:, None, :, :] * mask[None, :, None, :]).reshape(b, N_HEADS * t, w), \
        jnp.repeat(mask, t, axis=0).astype(F32)


def _collapse_heads(o, t):
    b, _, w = o.shape
    return o.reshape(b, N_HEADS, t, w).sum(axis=1)


def _pad_rows(x, rows):
    return jnp.pad(x, ((0, 0), (0, rows - x.shape[1]), (0, 0)))


def _sb_sample(q16, k_new, v_new, cache_k, cache_v, page_table):
    b, t, w = q16.shape
    n_pages = page_table.shape[1]
    page = cache_k.shape[1]
    q_all, head_mask = _expand_heads(q16)
    rows = q_all.shape[1]
    assert page == LANES
    kn = _pad_rows(k_new, LANES)
    vn = _pad_rows(v_new, LANES)
    per_b = lambda shp: pl.BlockSpec((1,) + shp, lambda bi, s, pt: (bi, 0, 0))
    const = lambda shp: pl.BlockSpec(shp, lambda bi, s, pt: (0, 0))
    grid_spec = pltpu.PrefetchScalarGridSpec(
        num_scalar_prefetch=1,
        grid=(b, n_pages // PAGES_PER_STEP),
        in_specs=[per_b((rows, w)), per_b((LANES, w)), per_b((LANES, w)), const((page, page)),
                  const((rows, w))]
        + _page_specs(n_pages, page, w, True) + _page_specs(n_pages, page, w, True),
        out_specs=per_b((rows, w)),
        scratch_shapes=[pltpu.VMEM((rows, LANES), F32), pltpu.VMEM((rows, w), F32)],
    )
    o = pl.pallas_call(
        functools.partial(_sb_sample_body, page=page, t_new=t),
        out_shape=jax.ShapeDtypeStruct((b, rows, w), F32),
        grid_spec=grid_spec,
        compiler_params=_params(("arbitrary", "arbitrary")),
        name="sb_sample",
    )(page_table, q_all, kn, vn, _tri(page), head_mask, *([cache_k] * PAGES_PER_STEP),
      *([cache_v] * PAGES_PER_STEP))
    return _collapse_heads(o, t)


def _kmean_sample_body(pt_ref, *refs):
    k_refs = refs[:PAGES_PER_STEP]
    o_ref = refs[PAGES_PER_STEP]
    pages_per_block = MOBA_BLOCK // k_refs[0].shape[0]
    for blk in range(PAGES_PER_STEP // pages_per_block):
        total = None
        for i in range(pages_per_block):
            part = jnp.sum(k_refs[blk * pages_per_block + i][...], axis=0, keepdims=True)
            total = part if total is None else total + part
        o_ref[0, 0, pl.ds(blk, 1), :] = total * (1.0 / MOBA_BLOCK)


def _kmean_sample(cache_k, page_table):
    b, n_pages = page_table.shape
    page, w = cache_k.shape[1:]
    steps = n_pages // PAGES_PER_STEP
    blocks_per_step = PAGES_PER_STEP * page // MOBA_BLOCK
    grid_spec = pltpu.PrefetchScalarGridSpec(
        num_scalar_prefetch=1,
        grid=(b, steps),
        in_specs=_page_specs(n_pages, page, w, False),
        out_specs=pl.BlockSpec((1, 1, blocks_per_step, w), lambda bi, s, pt: (bi, s, 0, 0)),
    )
    out = pl.pallas_call(
        _kmean_sample_body,
        out_shape=jax.ShapeDtypeStruct((b, steps, blocks_per_step, w), F32),
        grid_spec=grid_spec,
        compiler_params=_params(("arbitrary", "arbitrary")),
        name="kmean_sample",
    )(page_table, *([cache_k] * PAGES_PER_STEP))
    return out.reshape(b, steps * blocks_per_step, w)


def _moba_sample_body(pt_ref, q_ref, kn_ref, vn_ref, km_ref, hm_ref, *rest, page, t_new, n_blocks):
    k_refs = rest[:PAGES_PER_STEP]
    v_refs = rest[PAGES_PER_STEP:2 * PAGES_PER_STEP]
    o_ref = rest[2 * PAGES_PER_STEP]
    sel_ref, m_ref, l_ref, acc_ref = rest[2 * PAGES_PER_STEP + 1:]
    step = pl.program_id(1)
    q = q_ref[0]
    rows = q.shape[0]
    lane = lax.broadcasted_iota(jnp.int32, (rows, LANES), 1)
    lane_f = lane.astype(F32)

    def online(s, allowed, v, m, l, acc):
        s = jnp.where(allowed, s, NEG_BIG)
        m_new = jnp.maximum(m, jnp.max(s, axis=1, keepdims=True))
        p = jnp.where(allowed, jnp.exp(s - m_new), 0.0)
        alpha = jnp.exp(m - m_new)
        return m_new, alpha * l + jnp.sum(p, axis=1, keepdims=True), alpha * acc + _dot(p.astype(BF16), v)

    @pl.when(step == 0)
    def _():
        km_hi, km_lo = _split(km_ref[0])
        gate = _dot_nt(q, km_hi) + _dot_nt(q, km_lo)
        gate = jnp.where(lane < n_blocks, gate, -jnp.inf)
        chosen = jnp.zeros((rows, LANES), F32)
        for _k in range(min(MOBA_TOPK, n_blocks)):
            best = jnp.max(gate, axis=1, keepdims=True)
            first = jnp.min(jnp.where(gate == best, lane_f, float(LANES)), axis=1, keepdims=True)
            pick = lane_f == first
            chosen = jnp.where(pick, 1.0, chosen)
            gate = jnp.where(pick, -jnp.inf, gate)
        sel_ref[...] = chosen
        kn = kn_ref[0].astype(BF16)
        vn = vn_ref[0].astype(BF16)
        qt = lax.broadcasted_iota(jnp.int32, (rows, LANES), 0) % t_new
        m, l, acc = online(_dot_nt(q, kn), lane <= qt, vn, jnp.full((rows, 1), NEG_BIG, F32),
                           jnp.zeros((rows, 1), F32), jnp.zeros((rows, q.shape[1]), F32))
        m_ref[...] = jnp.broadcast_to(m, (rows, LANES))
        l_ref[...] = jnp.broadcast_to(l, (rows, LANES))
        acc_ref[...] = acc

    m = m_ref[:, 0:1]
    l = l_ref[:, 0:1]
    acc = acc_ref[...]
    chosen = sel_ref[...]
    pages_per_block = MOBA_BLOCK // page
    for i in range(PAGES_PER_STEP):
        blk = (step * PAGES_PER_STEP + i) // pages_per_block
        picked = jnp.sum(jnp.where(lane == blk, chosen, 0.0), axis=1, keepdims=True)
        allowed = jnp.broadcast_to(picked, (rows, page)) > 0.5
        k_pg = k_refs[i][...].astype(BF16)
        v_pg = v_refs[i][...].astype(BF16)
        m, l, acc = online(_dot_nt(q, k_pg), allowed, v_pg, m, l, acc)
    m_ref[...] = jnp.broadcast_to(m, (rows, LANES))
    l_ref[...] = jnp.broadcast_to(l, (rows, LANES))
    acc_ref[...] = acc

    @pl.when(step == pl.num_programs(1) - 1)
    def _():
        o_ref[0] = (acc / l) * hm_ref[...]


def _moba_sample(q16, k_new, v_new, kmean, cache_k, cache_v, page_table):
    b, t, w = q16.shape
    n_pages = page_table.shape[1]
    page = cache_k.shape[1]
    n_blocks = kmean.shape[1]
    q_all, head_mask = _expand_heads(q16)
    rows = q_all.shape[1]
    assert page == LANES and n_blocks <= LANES
    kn = _pad_rows(k_new, LANES)
    vn = _pad_rows(v_new, LANES)
    km = _pad_rows(kmean, LANES)
    specs = _page_specs(n_pages, page, w, False)
    per_b = lambda shp: pl.BlockSpec((1,) + shp, lambda bi, s, pt: (bi, 0, 0))
    const = lambda shp: pl.BlockSpec(shp, lambda bi, s, pt: (0, 0))
    grid_spec = pltpu.PrefetchScalarGridSpec(
        num_scalar_prefetch=1,
        grid=(b, n_pages // PAGES_PER_STEP),
        in_specs=[per_b((rows, w)), per_b((LANES, w)), per_b((LANES, w)), per_b((LANES, w)),
                  const((rows, w))] + specs + specs,
        out_specs=per_b((rows, w)),
        scratch_shapes=[pltpu.VMEM((rows, LANES), F32), pltpu.VMEM((rows, LANES), F32),
                        pltpu.VMEM((rows, LANES), F32), pltpu.VMEM((rows, w), F32)],
    )
    o = pl.pallas_call(
        functools.partial(_moba_sample_body, page=page, t_new=t, n_blocks=n_blocks),
        out_shape=jax.ShapeDtypeStruct((b, rows, w), F32),
        grid_spec=grid_spec,
        compiler_params=_params(("arbitrary", "arbitrary")),
        name="moba_sample",
    )(page_table, q_all, kn, vn, km, head_mask, *([cache_k] * PAGES_PER_STEP),
      *([cache_v] * PAGES_PER_STEP))
    return _collapse_heads(o, t)


def _merge_body(osb_ref, omb_ref, sgsb_ref, sgmb_ref, x_ref, ga_ref, sh_ref, sc_ref, g2_ref, wsb_ref,
                wmb_ref, wout_ref, wr_ref, br_ref, x1_ref, h2_ref, eid_ref, gate_ref):
    u = (sgsb_ref[0].astype(F32) * _dot(osb_ref[0], wsb_ref[...])
         + sgmb_ref[0].astype(F32) * _dot(omb_ref[0], wmb_ref[...]))
    x1 = x_ref[0] + ga_ref[0] * _dot(u.astype(BF16), wout_ref[...])
    x1_ref[0] = x1
    h2 = _modulate(x1, g2_ref[...], sh_ref[0], sc_ref[0])
    h2_ref[0] = h2
    logits = _dot3(h2, wr_ref[...]) + br_ref[...]
    rows = logits.shape[0]
    lane = lax.broadcasted_iota(jnp.int32, (rows, LANES), 1)
    lane_f = lane.astype(F32)
    eid = jnp.zeros((rows, LANES), F32)
    top = jnp.zeros((rows, LANES), F32)
    best0 = None
    for k in range(TOP_K):
        best = jnp.max(logits, axis=1, keepdims=True)
        first = jnp.min(jnp.where(logits == best, lane_f, float(LANES)), axis=1, keepdims=True)
        logits = jnp.where(lane_f == first, -jnp.inf, logits)
        if best0 is None:
            best0 = best
        eid = jnp.where(lane == k, first, eid)
        top = jnp.where(lane == k, jnp.exp(best - best0), top)
    eid_ref[0] = eid.astype(jnp.int32)
    gate_ref[0] = top / jnp.sum(top, axis=1, keepdims=True)


def _merge(o_sb, o_mb, sg_sb, sg_mb, x, ga1, sh2, sc2, g2, w_sb16, w_mb16, w_out16, w_router_pad,
           b_router_pad, tm):
    b, t, d = x.shape
    r = ga1.shape[1]
    nt = t // tm
    tile = lambda wd: pl.BlockSpec((1, tm, wd), lambda bi, ti: (bi, ti, 0))
    const = lambda shp: pl.BlockSpec(shp, lambda bi, ti: (0,) * len(shp))
    mod = pl.BlockSpec((1, r, d), lambda bi, ti: (bi, 0, 0))
    return pl.pallas_call(
        _merge_body,
        out_shape=[jax.ShapeDtypeStruct((b, t, d), F32), jax.ShapeDtypeStruct((b, t, d), F32),
                   jax.ShapeDtypeStruct((b, t, LANES), jnp.int32), jax.ShapeDtypeStruct((b, t, LANES), F32)],
        grid=(b, nt),
        in_specs=[tile(WIDTH), tile(WIDTH), tile(d), tile(d), tile(d), mod, mod, mod, const((1, d)),
                  const((WIDTH, d)), const((WIDTH, d)), const((d, d)), const((d, LANES)), const((1, LANES))],
        out_specs=[tile(d), tile(d), tile(LANES), tile(LANES)],
        compiler_params=_params(("arbitrary", "arbitrary")),
        name="merge",
    )(o_sb, o_mb, sg_sb, sg_mb, x, ga1, sh2, sc2, g2, w_sb16, w_mb16, w_out16, w_router_pad, b_router_pad)


def _ffn_body(be_ref, nu_ref, tok_ref, tokn_ref, dst_ref, h2_hbm, wgu_ref, bgu_ref, wd_ref, bd_ref, y_hbm,
              xbuf, obuf, wgu16, wd16, gsem, ssem, *, tm, ff):
    j = pl.program_id(0)
    n_used = nu_ref[0]
    cur = j % 2

    def gather_copy(src_row, r, buf):
        return pltpu.make_async_copy(h2_hbm.at[pl.ds(src_row, 1)], xbuf.at[buf, pl.ds(r, 1)], gsem.at[buf])

    def scatter_copy(r, dst_row, buf):
        return pltpu.make_async_copy(obuf.at[buf, pl.ds(r, 1)], y_hbm.at[pl.ds(dst_row, 1)], ssem.at[buf])

    def start_gather(idx_ref, buf):
        def body(r, c):
            gather_copy(idx_ref[0, 0, r], r, buf).start()
            return c
        lax.fori_loop(0, tm, body, 0, unroll=8)

    def wait_gather(buf):
        for r in range(tm):
            gather_copy(0, r, buf).wait()

    def start_scatter(buf):
        def body(r, c):
            scatter_copy(r, dst_ref[0, 0, r], buf).start()
            return c
        lax.fori_loop(0, tm, body, 0, unroll=8)

    def wait_scatter(buf):
        for r in range(tm):
            scatter_copy(r, 0, buf).wait()

    @pl.when(j == 0)
    def _():
        n_slots = y_hbm.shape[0] - 2 * tm
        obuf[0] = jnp.zeros((tm, obuf.shape[2]), F32)
        for half in range(2):
            fill = pltpu.make_async_copy(obuf.at[0], y_hbm.at[pl.ds(n_slots + half * tm, tm)], ssem.at[0])
            fill.start()
            fill.wait()

    @pl.when(j < n_used)
    def _():
        @pl.when(j == 0)
        def _():
            start_gather(tok_ref, 0)

        @pl.when(j + 1 < n_used)
        def _():
            start_gather(tokn_ref, 1 - cur)

        @pl.when((j == 0) | (be_ref[j] != be_ref[jnp.maximum(j - 1, 0)]))
        def _():
            wgu16[...] = wgu_ref[0].astype(BF16)
            wd16[...] = wd_ref[0].astype(BF16)

        wait_gather(cur)
        x16 = xbuf[cur].astype(BF16)
        y = jnp.zeros((tm, wd16.shape[1]), F32) + bd_ref[0]
        chunk = 512
        for c in range(ff // chunk):
            g = _dot(x16, wgu16[:, c * chunk:(c + 1) * chunk]) + bgu_ref[0, :, c * chunk:(c + 1) * chunk]
            u = (_dot(x16, wgu16[:, ff + c * chunk:ff + (c + 1) * chunk])
                 + bgu_ref[0, :, ff + c * chunk:ff + (c + 1) * chunk])
            g = jnp.minimum(g, SWIGLU_LIMIT)
            u = jnp.clip(u, -SWIGLU_LIMIT, SWIGLU_LIMIT)
            act = (u + 1.0) * (g * jax.nn.sigmoid(SWIGLU_ALPHA * g))
            y = y + _dot(act.astype(BF16), wd16[c * chunk:(c + 1) * chunk, :])

        @pl.when(j >= 2)
        def _():
            wait_scatter(cur)

        obuf[cur] = y
        start_scatter(cur)

        @pl.when(j == n_used - 1)
        def _():
            wait_scatter(cur)

            @pl.when(j >= 1)
            def _():
                wait_scatter(1 - cur)


def _expert_ffn(h2_rows, row_tok, row_dst, block_e, n_used, w_gu, b_gu, w_down, b_down, n_out_rows):
    n_blocks, _, tm = row_tok.shape
    e, d, ff2 = w_gu.shape
    ff = ff2 // 2

    def blk(j, be, nu):
        return (jnp.minimum(j, nu[0] - 1), 0, 0)

    def blk_next(j, be, nu):
        return (jnp.minimum(j + 1, nu[0] - 1), 0, 0)

    smem = lambda index: pl.BlockSpec((1, 1, tm), index, memory_space=pltpu.SMEM)
    per_e = lambda shp: pl.BlockSpec((1,) + shp, lambda j, be, nu: (be[j], 0, 0))
    grid_spec = pltpu.PrefetchScalarGridSpec(
        num_scalar_prefetch=2,
        grid=(n_blocks,),
        in_specs=[smem(blk), smem(blk_next), smem(blk), pl.BlockSpec(memory_space=pl.ANY),
                  per_e((d, ff2)), per_e((1, ff2)), per_e((ff, d)), per_e((1, d))],
        out_specs=pl.BlockSpec(memory_space=pl.ANY),
        scratch_shapes=[pltpu.VMEM((2, tm, d), F32), pltpu.VMEM((2, tm, d), F32),
                        pltpu.VMEM((d, ff2), BF16), pltpu.VMEM((ff, d), BF16),
                        pltpu.SemaphoreType.DMA((2,)), pltpu.SemaphoreType.DMA((2,))],
    )
    return pl.pallas_call(
        functools.partial(_ffn_body, tm=tm, ff=ff),
        out_shape=jax.ShapeDtypeStruct((n_out_rows, d), F32),
        grid_spec=grid_spec,
        compiler_params=_params(("arbitrary",), vmem=56 * 1024 * 1024),
        name="expert_ffn",
    )(block_e, n_used, row_tok, row_tok, row_dst, h2_rows, w_gu, b_gu.reshape(e, 1, ff2), w_down,
      b_down.reshape(e, 1, d))


def _route(eid, n_tokens):
    tile = EXPERT_TILE
    nk = n_tokens * TOP_K
    n_blocks = (nk + N_EXPERTS * (tile - 1) + tile - 1) // tile
    rows = n_blocks * tile
    flat_e = eid.reshape(-1)
    order = jnp.argsort(flat_e, stable=True).astype(jnp.int32)
    e_sorted = flat_e[order]
    counts = jnp.bincount(flat_e, length=N_EXPERTS)
    padded = (counts + tile - 1) // tile * tile
    start = jnp.cumsum(counts) - counts
    pend = jnp.cumsum(padded)
    pstart = pend - padded
    dest = (pstart[e_sorted] + jnp.arange(nk) - start[e_sorted]).astype(jnp.int32)
    pos = jnp.arange(rows, dtype=jnp.int32)
    dump = nk + ((pos // tile) % 2) * tile + pos % tile
    row_slot = dump.at[dest].set(order)
    row_tok = jnp.where(row_slot < nk, row_slot // TOP_K, n_tokens).astype(jnp.int32)
    n_used = (pend[-1] // tile).astype(jnp.int32)
    blk_start = jnp.minimum(jnp.arange(n_blocks), n_used - 1) * tile
    block_e = jnp.minimum(jnp.searchsorted(pend, blk_start, side='right'), N_EXPERTS - 1).astype(jnp.int32)
    return (row_tok.reshape(n_blocks, 1, tile), row_slot.reshape(n_blocks, 1, tile), block_e,
            n_used.reshape(1), nk + 2 * tile)


def _combine_body(y4_ref, gate_ref, x1_ref, ga_ref, o_ref):
    d = x1_ref.shape[2]
    gate = gate_ref[0]
    ffn = None
    for k in range(TOP_K):
        part = gate[:, k:k + 1] * y4_ref[:, k * d:(k + 1) * d]
        ffn = part if ffn is None else ffn + part
    o_ref[0] = x1_ref[0] + ga_ref[0] * ffn


def _combine(y4, gate, x1, ga2, tm, first_block):
    b, t, d = x1.shape
    r = ga2.shape[1]
    nt = t // tm
    tile = lambda wd: pl.BlockSpec((1, tm, wd), lambda bi, ti: (bi, ti, 0))
    return pl.pallas_call(
        _combine_body,
        out_shape=jax.ShapeDtypeStruct((b, t, d), F32),
        grid=(b, nt),
        in_specs=[pl.BlockSpec((tm, TOP_K * d), lambda bi, ti: (first_block + bi * nt + ti, 0)),
                  tile(LANES), tile(d), pl.BlockSpec((1, r, d), lambda bi, ti: (bi, 0, 0))],
        out_specs=tile(d),
        compiler_params=_params(("arbitrary", "arbitrary")),
        name="combine",
    )(y4, gate, x1, ga2)


def _head_gain(g):
    return jnp.tile(g.astype(F32), N_HEADS).reshape(1, WIDTH)


def _mods(mod, d):
    return [mod[:, None, i * d:(i + 1) * d] for i in range(N_MOD)]


def kernel(x_prompt, x_sample, c_prompt, c_sample, cache_sb_k, cache_sb_v, cache_mb_k, cache_mb_v, page_table,
           w_ada, b_ada, g_norm1, g_norm2, w_in, q_norm_g, k_norm_g, w_br_sb, w_br_mb, w_out, w_router,
           b_router, w_gu, b_gu, w_down, b_down):
    depth = w_ada.shape[0]
    assert depth == 1
    bp, seq, d = x_prompt.shape
    bs, t_new, _ = x_sample.shape
    n_phys, page = cache_sb_k.shape[1:3]
    past = page_table.shape[1] * page
    n_prompt = bp * seq
    n_sample = bs * t_new
    lyr = 0

    mod = _adaln(jnp.concatenate([c_prompt, c_sample], axis=0), w_ada[lyr], b_ada[lyr])
    sh1p, sc1p, ga1p, sh2p, sc2p, ga2p = _mods(mod[:bp], d)
    per_row = lambda m: jnp.repeat(m, t_new, axis=0).reshape(1, n_sample, d)
    sh1s, sc1s, ga1s, sh2s, sc2s, ga2s = [per_row(m[:, 0]) for m in _mods(mod[bp:], d)]

    w_in16 = w_in[lyr].astype(BF16)
    g1 = g_norm1[lyr].reshape(1, d)
    g2 = g_norm2[lyr].reshape(1, d)
    qg = _head_gain(q_norm_g[lyr])
    kg = _head_gain(k_norm_g[lyr])
    head_of = np.arange(WIDTH) // HEAD_DIM
    gmat = jnp.asarray((head_of[:, None] == head_of[None, :]).astype(np.float32) / HEAD_DIM, dtype=BF16)

    pos_p = jnp.arange(seq, dtype=jnp.int32)
    (qsb, ksb, vsb, ksb16, vsb16, qmb, kmb, vmb, kmb16, vmb16, sgsb, sgmb, kmean) = _inproj(
        x_prompt, sh1p, sc1p, g1, w_in16, pos_p, qg, kg, gmat, ROW_TILE, True)
    o_sb_p = _sb_prompt(qsb, ksb16, vsb16)
    o_mb_p = _moba_prompt(qmb, kmb16, vmb16, kmean)

    xs = x_sample.reshape(1, n_sample, d)
    pos_s = past + jnp.arange(n_sample, dtype=jnp.int32) % t_new
    (qsb_s, ksb_s, vsb_s, _, _, qmb_s, kmb_s, vmb_s, _, _, sgsb_s, sgmb_s) = _inproj(
        xs, sh1s, sc1s, g1, w_in16, pos_s, qg, kg, gmat, n_sample, False)
    by_batch = lambda a: a.reshape(bs, t_new, WIDTH)
    flat_cache = lambda c: c[lyr].reshape(n_phys, page, WIDTH)
    o_sb_s = _sb_sample(by_batch(qsb_s), by_batch(ksb_s), by_batch(vsb_s), flat_cache(cache_sb_k),
                        flat_cache(cache_sb_v), page_table)
    kmean_s = _kmean_sample(flat_cache(cache_mb_k), page_table)
    o_mb_s = _moba_sample(by_batch(qmb_s), by_batch(kmb_s), by_batch(vmb_s), kmean_s,
                          flat_cache(cache_mb_k), flat_cache(cache_mb_v), page_table)
    o_sb_s = o_sb_s.reshape(1, n_sample, WIDTH).astype(BF16)
    o_mb_s = o_mb_s.reshape(1, n_sample, WIDTH).astype(BF16)

    w_sb16 = w_br_sb[lyr].astype(BF16)
    w_mb16 = w_br_mb[lyr].astype(BF16)
    w_out16 = w_out[lyr].astype(BF16)
    w_router_pad = jnp.pad(w_router[lyr], ((0, 0), (0, LANES - N_EXPERTS)))
    b_router_pad = jnp.concatenate([b_router[lyr].astype(F32),
                                    jnp.full((LANES - N_EXPERTS,), -jnp.inf, F32)]).reshape(1, LANES)
    x1p, h2p, eidp, gatep = _merge(o_sb_p, o_mb_p, sgsb, sgmb, x_prompt, ga1p, sh2p, sc2p, g2, w_sb16,
                                   w_mb16, w_out16, w_router_pad, b_router_pad, ROW_TILE)
    x1s, h2s, eids, gates = _merge(o_sb_s, o_mb_s, sgsb_s, sgmb_s, xs, ga1s, sh2s, sc2s, g2, w_sb16,
                                   w_mb16, w_out16, w_router_pad, b_router_pad, n_sample)

    n_tokens = n_prompt + n_sample
    h2_rows = jnp.concatenate([h2p.reshape(n_prompt, d), h2s.reshape(n_sample, d), jnp.zeros((8, d), F32)])
    eid = jnp.concatenate([eidp.reshape(n_prompt, LANES), eids.reshape(n_sample, LANES)])[:, :TOP_K]
    row_tok, row_dst, block_e, n_used, n_out_rows = _route(eid, n_tokens)
    y_rows = _expert_ffn(h2_rows, row_tok, row_dst, block_e, n_used, w_gu[lyr], b_gu[lyr], w_down[lyr],
                         b_down[lyr], n_out_rows)
    y4 = y_rows.reshape(n_out_rows // TOP_K, TOP_K * d)
    y_prompt = _combine(y4, gatep, x1p, ga2p, ROW_TILE, 0)
    y_sample = _combine(y4, gates, x1s, ga2s, n_sample, n_prompt // n_sample)

    heads = lambda a, b_, t_: a.reshape(1, b_, t_, N_HEADS, HEAD_DIM)
    return (y_prompt, y_sample.reshape(bs, t_new, d),
            heads(ksb, bp, seq), heads(vsb, bp, seq), heads(kmb, bp, seq), heads(vmb, bp, seq),
            heads(ksb_s, bs, t_new), heads(vsb_s, bs, t_new), heads(kmb_s, bs, t_new), heads(vmb_s, bs, t_new))
```

```python
import functools

import numpy as np
import jax
import jax.numpy as jnp
from jax import lax
from jax.experimental import pallas as pl
from jax.experimental.pallas import tpu as pltpu

HEAD_DIM = 64
N_HEADS = 8
WIDTH = N_HEADS * HEAD_DIM
MOBA_BLOCK = 256
MOBA_TOPK = 3
ROPE_THETA = 10000.0
N_EXPERTS = 32
TOP_K = 4
SWIGLU_LIMIT = 7.0
SWIGLU_ALPHA = 1.702
NORM_EPS = 1e-6
N_MOD = 6
ATTN_SCALE = HEAD_DIM ** -0.5

LANES = 128
SUBLANES = 8
ROW_TILE = 256
ATTN_TILE = 256
EXPERT_TILE = 512
COMBINE_TILE = 128
PAGES_PER_STEP = 8
NEG_BIG = -1e30
STICK_FLOOR = -104.0
VMEM_LIMIT = 48 * 1024 * 1024

F32 = jnp.float32
BF16 = jnp.bfloat16


def _dot(a, b):
    return jnp.dot(a, b, preferred_element_type=F32)


def _dot_nt(a, b):
    return lax.dot_general(a, b, (((1,), (1,)), ((), ())), preferred_element_type=F32)


def _split(x):
    hi = x.astype(BF16)
    lo = (x - hi.astype(F32)).astype(BF16)
    return hi, lo


def _dot3(a, b):
    a_hi, a_lo = _split(a)
    b_hi, b_lo = _split(b)
    return _dot(a_hi, b_hi) + _dot(a_lo, b_hi) + _dot(a_hi, b_lo)


def _params(sem, vmem=VMEM_LIMIT):
    return pltpu.CompilerParams(dimension_semantics=sem, vmem_limit_bytes=vmem)


def _adaln_body(c_ref, w_ref, b_ref, o_ref):
    c = c_ref[...]
    s = c * jax.nn.sigmoid(c)
    o_ref[...] = _dot3(s, w_ref[...]) + b_ref[...]


def _adaln(c, w_ada, b_ada):
    rows, d = c.shape
    cols = w_ada.shape[1]
    tn = 1024
    return pl.pallas_call(
        _adaln_body,
        out_shape=jax.ShapeDtypeStruct((rows, cols), F32),
        grid=(cols // tn,),
        in_specs=[pl.BlockSpec((rows, d), lambda j: (0, 0)),
                  pl.BlockSpec((d, tn), lambda j: (0, j)),
                  pl.BlockSpec((1, tn), lambda j: (0, j))],
        out_specs=pl.BlockSpec((rows, tn), lambda j: (0, j)),
        compiler_params=_params(("arbitrary",)),
        name="adaln",
    )(c, w_ada, b_ada.reshape(1, cols))


def _modulate(x, g, shift, scale):
    ms = jnp.mean(x * x, axis=-1, keepdims=True)
    return (x * lax.rsqrt(ms + NORM_EPS) * g) * (1.0 + scale) + shift


def _qk_norm_rope(p, g, gmat, cos, sin_signed):
    ms = _dot((p * p).astype(BF16), gmat)
    y = p * lax.rsqrt(ms + NORM_EPS) * g
    lane = lax.broadcasted_iota(jnp.int32, (p.shape[0], LANES), 1)
    first_half = (lane & (HEAD_DIM // 2)) == 0
    outs = []
    for c in range(WIDTH // LANES):
        yc = y[:, c * LANES:(c + 1) * LANES]
        partner = jnp.where(first_half, pltpu.roll(yc, LANES - HEAD_DIM // 2, 1),
                            pltpu.roll(yc, HEAD_DIM // 2, 1))
        outs.append(yc * cos + partner * sin_signed)
    return jnp.concatenate(outs, axis=1)


def _inproj_body(emit_kmean, x_ref, sh_ref, sc_ref, g1_ref, w_ref, cos_ref, sin_ref, qg_ref, kg_ref,
                 gmat_ref, qsb_ref, ksb_ref, vsb_ref, ksb16_ref, vsb16_ref, qmb_ref, kmb_ref, vmb_ref,
                 kmb16_ref, vmb16_ref, sgsb_ref, sgmb_ref, *rest):
    h = _modulate(x_ref[0], g1_ref[...], sh_ref[0], sc_ref[0])
    h16 = h.astype(BF16)

    def proj(c0, width):
        return _dot(h16, w_ref[:, c0:c0 + width])

    w = WIDTH
    qsb_ref[0] = (proj(0, w) * ATTN_SCALE).astype(BF16)
    ksb = proj(w, w)
    ksb_ref[0] = ksb
    ksb16_ref[0] = ksb.astype(BF16)
    vsb = proj(2 * w, w)
    vsb_ref[0] = vsb
    vsb16_ref[0] = vsb.astype(BF16)
    cos = cos_ref[...]
    sin = sin_ref[...]
    gmat = gmat_ref[...]
    qmb = _qk_norm_rope(proj(3 * w, w), qg_ref[...], gmat, cos, sin)
    qmb_ref[0] = (qmb * ATTN_SCALE).astype(BF16)
    kmb = _qk_norm_rope(proj(4 * w, w), kg_ref[...], gmat, cos, sin)
    kmb_ref[0] = kmb
    kmb16_ref[0] = kmb.astype(BF16)
    vmb = proj(5 * w, w)
    vmb_ref[0] = vmb
    vmb16_ref[0] = vmb.astype(BF16)
    d = x_ref.shape[2]
    sgsb_ref[0] = jax.nn.sigmoid(proj(6 * w, d)).astype(BF16)
    sgmb_ref[0] = jax.nn.sigmoid(proj(6 * w + d, d)).astype(BF16)
    if emit_kmean:
        kmean_ref = rest[0]
        ti = pl.program_id(1)
        kmean_ref[0, pl.ds(ti, 1), :] = jnp.mean(kmb, axis=0, keepdims=True)


def _rope_tables(pos):
    inv_freq = ROPE_THETA ** (-jnp.arange(0, HEAD_DIM, 2, dtype=F32) / HEAD_DIM)
    ang = pos.astype(F32)[:, None] * inv_freq[None, :]
    cos = jnp.tile(jnp.cos(ang), (1, LANES // (HEAD_DIM // 2)))
    sin = jnp.sin(ang)
    sin_signed = jnp.tile(jnp.concatenate([-sin, sin], axis=1), (1, LANES // HEAD_DIM))
    return cos, sin_signed


def _inproj(x, shift, scale, g1, w_in16, pos, qg, kg, gmat, tm, emit_kmean):
    b, t, d = x.shape
    r = shift.shape[1]
    cols = w_in16.shape[1]
    nt = t // tm
    cos, sin_signed = _rope_tables(pos)
    row = lambda dt, wd: jax.ShapeDtypeStruct((b, t, wd), dt)
    tile = lambda wd: pl.BlockSpec((1, tm, wd), lambda bi, ti: (bi, ti, 0))
    const = lambda shp: pl.BlockSpec(shp, lambda bi, ti: (0,) * len(shp))
    mod = pl.BlockSpec((1, r, d), lambda bi, ti: (bi, 0, 0))
    out_shape = [row(BF16, WIDTH), row(F32, WIDTH), row(F32, WIDTH), row(BF16, WIDTH), row(BF16, WIDTH),
                 row(BF16, WIDTH), row(F32, WIDTH), row(F32, WIDTH), row(BF16, WIDTH), row(BF16, WIDTH),
                 row(BF16, d), row(BF16, d)]
    out_specs = [tile(WIDTH)] * 10 + [tile(d)] * 2
    if emit_kmean:
        assert tm == MOBA_BLOCK
        out_shape.append(jax.ShapeDtypeStruct((b, nt, WIDTH), F32))
        out_specs.append(pl.BlockSpec((1, nt, WIDTH), lambda bi, ti: (bi, 0, 0)))
    return pl.pallas_call(
        functools.partial(_inproj_body, emit_kmean),
        out_shape=out_shape,
        grid=(b, nt),
        in_specs=[tile(d), mod, mod, const((1, d)), const((d, cols)),
                  pl.BlockSpec((tm, LANES), lambda bi, ti: (ti, 0)),
                  pl.BlockSpec((tm, LANES), lambda bi, ti: (ti, 0)),
                  const((1, WIDTH)), const((1, WIDTH)), const((WIDTH, WIDTH))],
        out_specs=out_specs,
        compiler_params=_params(("arbitrary", "arbitrary")),
        name="inproj_kmean" if emit_kmean else "inproj",
    )(x, shift, scale, g1, w_in16, cos, sin_signed, qg, kg, gmat)


def _stick_tile(z, allowed, tri, carry):
    sp = jnp.maximum(z, 0.0) + jnp.log(1.0 + jnp.exp(-jnp.abs(z)))
    log_keep = -sp
    if allowed is not None:
        log_keep = jnp.where(allowed, log_keep, 0.0)
    lk_hi, lk_lo = _split(log_keep)
    later = _dot(lk_hi, tri) + _dot(lk_lo, tri)
    ones = jnp.ones((z.shape[1], LANES), BF16)
    total = _dot(lk_hi, ones) + _dot(lk_lo, ones)
    reps = z.shape[1] // LANES
    carry_wide = carry if reps == 1 else jnp.concatenate([carry] * reps, axis=1)
    a = jnp.exp((z - sp) + later + carry_wide)
    if allowed is not None:
        a = jnp.where(allowed, a, 0.0)
    return a, carry + total


def _head_masks(rows):
    lane = lax.broadcasted_iota(jnp.int32, (rows, LANES), 1)
    return lane < HEAD_DIM


def _sb_prompt_body(q_ref, k_ref, v_ref, tri_ref, o_ref, *, tile):
    qi = pl.program_id(2)
    q = q_ref[0]
    tri = tri_ref[...]
    low = _head_masks(tile)
    row = lax.broadcasted_iota(jnp.int32, (tile, tile), 0)
    col = lax.broadcasted_iota(jnp.int32, (tile, tile), 1)
    diag_allowed = col < row
    zeros16 = jnp.zeros_like(q)
    qms = (jnp.where(low, q, zeros16), jnp.where(low, zeros16, q))

    def key_tile(kb, state, allowed):
        start = pl.multiple_of(kb * tile, tile)
        k_blk = k_ref[0, pl.ds(start, tile), :]
        v_blk = v_ref[0, pl.ds(start, tile), :]
        out = []
        for h in range(2):
            carry, acc = state[h]
            a, carry = _stick_tile(_dot_nt(qms[h], k_blk), allowed, tri, carry)
            out.append((carry, acc + _dot(a.astype(BF16), v_blk)))
        return tuple(out)

    zero = jnp.zeros((tile, LANES), F32)
    state = key_tile(qi, ((zero, zero), (zero, zero)), diag_allowed)

    def live(st):
        return jnp.maximum(jnp.max(st[0][0]), jnp.max(st[1][0])) > STICK_FLOOR

    def cond(c):
        return (c[0] < qi) & live(c[1])

    def body(c):
        return c[0] + 1, key_tile(qi - 1 - c[0], c[1], None)

    _, state = lax.while_loop(cond, body, (jnp.int32(0), state))
    o_ref[0] = jnp.where(low, state[0][1], state[1][1]).astype(o_ref.dtype)


def _tri(n):
    j = np.arange(n)[:, None]
    s = np.arange(n)[None, :]
    return jnp.asarray((j > s).astype(np.float32), dtype=BF16)


def _sb_prompt(q16, k16, v16):
    b, s, w = q16.shape
    tile = ATTN_TILE
    pairs = w // LANES
    return pl.pallas_call(
        functools.partial(_sb_prompt_body, tile=tile),
        out_shape=jax.ShapeDtypeStruct((b, s, w), BF16),
        grid=(b, pairs, s // tile),
        in_specs=[pl.BlockSpec((1, tile, LANES), lambda bi, p, qi: (bi, qi, p)),
                  pl.BlockSpec((1, s, LANES), lambda bi, p, qi: (bi, 0, p)),
                  pl.BlockSpec((1, s, LANES), lambda bi, p, qi: (bi, 0, p)),
                  pl.BlockSpec((tile, tile), lambda bi, p, qi: (0, 0))],
        out_specs=pl.BlockSpec((1, tile, LANES), lambda bi, p, qi: (bi, qi, p)),
        compiler_params=_params(("arbitrary",) * 3),
        name="sb_prompt",
    )(q16, k16, v16, _tri(tile))


def _moba_prompt_body(q_ref, k_ref, v_ref, km_ref, o_ref, sel_ref, *, tile, nb):
    own = pl.program_id(2)
    q = q_ref[0]
    km = km_ref[0]
    km_pad = jnp.concatenate([km, jnp.zeros((LANES - nb, LANES), F32)], axis=0)
    km_hi, km_lo = _split(km_pad)
    low = _head_masks(tile)
    lane = lax.broadcasted_iota(jnp.int32, (tile, LANES), 1)
    row = lax.broadcasted_iota(jnp.int32, (tile, tile), 0)
    col = lax.broadcasted_iota(jnp.int32, (tile, tile), 1)
    causal = col <= row
    past = lane < own
    zeros16 = jnp.zeros_like(q)
    qms = (jnp.where(low, q, zeros16), jnp.where(low, zeros16, q))
    ones = jnp.ones((tile, LANES), BF16)
    for h in range(2):
        gate = _dot_nt(qms[h], km_hi) + _dot_nt(qms[h], km_lo)
        gate = jnp.where(past, gate, -jnp.inf)
        for n in range(nb):
            g_n = gate[:, n:n + 1]
            ahead = jnp.where(gate > g_n, 1.0, jnp.where((gate == g_n) & (lane < n), 1.0, 0.0))
            rank = jnp.sum(jnp.where(past, ahead, 0.0), axis=1, keepdims=True)
            chosen = jnp.where(rank < MOBA_TOPK, 1.0, 0.0)
            sel_ref[h, n] = jnp.broadcast_to(chosen, (tile, LANES))

    def key_tile(kb, state, allowed):
        start = pl.multiple_of(kb * tile, tile)
        k_blk = k_ref[0, pl.ds(start, tile), :]
        v_blk = v_ref[0, pl.ds(start, tile), :]
        out = []
        for h in range(2):
            m, l, acc = state[h]
            ok = allowed
            if ok is None:
                ok = jnp.concatenate([sel_ref[h, kb]] * (tile // LANES), axis=1) > 0.5
            s = jnp.where(ok, _dot_nt(qms[h], k_blk), NEG_BIG)
            m_new = jnp.maximum(m, jnp.max(s, axis=1, keepdims=True))
            p = jnp.where(ok, jnp.exp(s - m_new), 0.0).astype(BF16)
            alpha = jnp.exp(m - m_new)
            l = alpha * l + _dot(p, ones)
            acc = alpha * acc + _dot(p, v_blk)
            out.append((m_new, l, acc))
        return tuple(out)

    m0 = jnp.full((tile, 1), NEG_BIG, F32)
    z0 = jnp.zeros((tile, LANES), F32)
    state = key_tile(own, ((m0, z0, z0), (m0, z0, z0)), causal)
    state = lax.fori_loop(0, own, lambda kb, st: key_tile(kb, st, None), state)
    outs = [state[h][2] / state[h][1] for h in range(2)]
    o_ref[0] = jnp.where(low, outs[0], outs[1]).astype(o_ref.dtype)


def _moba_prompt(q16, k16, v16, kmean):
    b, s, w = q16.shape
    tile = MOBA_BLOCK
    nb = s // tile
    pairs = w // LANES
    return pl.pallas_call(
        functools.partial(_moba_prompt_body, tile=tile, nb=nb),
        out_shape=jax.ShapeDtypeStruct((b, s, w), BF16),
        grid=(b, pairs, nb),
        in_specs=[pl.BlockSpec((1, tile, LANES), lambda bi, p, qi: (bi, qi, p)),
                  pl.BlockSpec((1, s, LANES), lambda bi, p, qi: (bi, 0, p)),
                  pl.BlockSpec((1, s, LANES), lambda bi, p, qi: (bi, 0, p)),
                  pl.BlockSpec((1, nb, LANES), lambda bi, p, qi: (bi, 0, p))],
        out_specs=pl.BlockSpec((1, tile, LANES), lambda bi, p, qi: (bi, qi, p)),
        scratch_shapes=[pltpu.VMEM((2, nb, tile, LANES), F32)],
        compiler_params=_params(("arbitrary",) * 3),
        name="moba_prompt",
    )(q16, k16, v16, kmean)


def _pages_t(cache, layer):
    n_phys, page = cache.shape[1:3]
    return jnp.transpose(cache[layer], (0, 2, 3, 1)).reshape(n_phys, WIDTH, page)


def _page_specs(n_pages, page, width):
    specs = []
    for i in range(PAGES_PER_STEP):
        def index(bi, s, pt, i=i):
            return (pt[bi, s * PAGES_PER_STEP + i], 0, 0)
        specs.append(pl.BlockSpec((None, width, page), index))
    return specs


def _expand_heads(q):
    b, t, w = q.shape
    head_of_lane = jnp.arange(w) // HEAD_DIM
    mask = (head_of_lane[None, :] == jnp.arange(N_HEADS)[:, None]).astype(q.dtype)
    return (q[:, None, :, :] * mask[None, :, None, :]).reshape(b, N_HEADS * t, w), \
        jnp.repeat(mask, t, axis=0).astype(F32)


def _collapse_heads(o, t):
    b, _, w = o.shape
    return o.reshape(b, N_HEADS, t, w).sum(axis=1)


def _pad_rows(x, rows):
    return jnp.pad(x, ((0, 0), (0, rows - x.shape[1]), (0, 0)))


def _sb_sample_body(pt_ref, q_ref, kn_ref, vn_ref, tri_ref, hm_ref, kc_hbm, vc_hbm, o_ref, kbuf, vbuf, sem,
                    *, n_pages, t_new):
    b = pl.program_id(0)
    q = q_ref[0]
    rows = q.shape[0]
    tri = tri_ref[...]

    def page_copies(bi, p, slot):
        phys = pt_ref[bi, p]
        return (pltpu.make_async_copy(kc_hbm.at[phys], kbuf.at[slot], sem.at[0, slot]),
                pltpu.make_async_copy(vc_hbm.at[phys], vbuf.at[slot], sem.at[1, slot]))

    def start(bi, p, slot):
        for c in page_copies(bi, p, slot):
            c.start()

    def wait(bi, p, slot):
        for c in page_copies(bi, p, slot):
            c.wait()

    def slot_of(p):
        return (n_pages - 1 - p) % 2

    @pl.when(b == 0)
    def _():
        start(b, n_pages - 1, 0)

    kn = kn_ref[0].astype(BF16)
    vn = vn_ref[0].astype(BF16)
    key = lax.broadcasted_iota(jnp.int32, (rows, LANES), 1)
    qt = lax.broadcasted_iota(jnp.int32, (rows, LANES), 0) % t_new
    a, carry = _stick_tile(_dot_nt(q, kn), key < qt, tri, jnp.zeros((rows, LANES), F32))
    acc = _dot(a.astype(BF16), vn)

    def cond(c):
        return (c[0] >= 0) & (jnp.max(c[1]) > STICK_FLOOR)

    def body(c):
        p, carry, acc = c
        slot = slot_of(p)
        wait(b, p, slot)

        @pl.when(p >= 1)
        def _():
            start(b, p - 1, 1 - slot)

        k_t = kbuf[slot].astype(BF16)
        v_t = vbuf[slot].astype(BF16)
        a, carry = _stick_tile(_dot(q, k_t), None, tri, carry)
        return p - 1, carry, acc + _dot_nt(a.astype(BF16), v_t)

    p, carry, acc = lax.while_loop(cond, body, (jnp.int32(n_pages - 1), carry, acc))

    @pl.when(p >= 0)
    def _():
        wait(b, p, slot_of(p))

    @pl.when(b + 1 < pl.num_programs(0))
    def _():
        start(b + 1, n_pages - 1, 0)

    o_ref[0] = acc * hm_ref[...]


def _sb_sample(q16, k_new, v_new, cache_kt, cache_vt, page_table):
    b, t, w = q16.shape
    n_pages = page_table.shape[1]
    page = cache_kt.shape[2]
    q_all, head_mask = _expand_heads(q16)
    rows = q_all.shape[1]
    assert page == LANES
    kn = _pad_rows(k_new, LANES)
    vn = _pad_rows(v_new, LANES)
    per_b = lambda shp: pl.BlockSpec((1,) + shp, lambda bi, pt: (bi, 0, 0))
    const = lambda shp: pl.BlockSpec(shp, lambda bi, pt: (0, 0))
    grid_spec = pltpu.PrefetchScalarGridSpec(
        num_scalar_prefetch=1,
        grid=(b,),
        in_specs=[per_b((rows, w)), per_b((LANES, w)), per_b((LANES, w)), const((page, page)),
                  const((rows, w)), pl.BlockSpec(memory_space=pl.ANY), pl.BlockSpec(memory_space=pl.ANY)],
        out_specs=per_b((rows, w)),
        scratch_shapes=[pltpu.VMEM((2, w, page), F32), pltpu.VMEM((2, w, page), F32),
                        pltpu.SemaphoreType.DMA((2, 2))],
    )
    o = pl.pallas_call(
        functools.partial(_sb_sample_body, n_pages=n_pages, t_new=t),
        out_shape=jax.ShapeDtypeStruct((b, rows, w), F32),
        grid_spec=grid_spec,
        compiler_params=_params(("arbitrary",)),
        name="sb_sample",
    )(page_table, q_all, kn, vn, _tri(page), head_mask, cache_kt, cache_vt)
    return _collapse_heads(o, t)


def _moba_select_body(pt_ref, q_ref, *refs, n_blocks):
    k_refs = refs[:PAGES_PER_STEP]
    o_ref = refs[PAGES_PER_STEP]
    km_ref = refs[PAGES_PER_STEP + 1]
    step = pl.program_id(1)
    page = k_refs[0].shape[1]
    pages_per_block = MOBA_BLOCK // page
    lane = lax.broadcasted_iota(jnp.int32, km_ref.shape, 1)

    @pl.when(step == 0)
    def _():
        km_ref[...] = jnp.zeros(km_ref.shape, F32)

    km = km_ref[...]
    for blk in range(PAGES_PER_STEP // pages_per_block):
        total = k_refs[blk * pages_per_block][...]
        for i in range(1, pages_per_block):
            total = total + k_refs[blk * pages_per_block + i][...]
        mean = jnp.sum(total, axis=1, keepdims=True) * (1.0 / MOBA_BLOCK)
        km = jnp.where(lane == step * (PAGES_PER_STEP // pages_per_block) + blk, mean, km)
    km_ref[...] = km

    @pl.when(step == pl.num_programs(1) - 1)
    def _():
        q = q_ref[0]
        rows = q.shape[0]
        km_hi, km_lo = _split(km)
        gate = _dot(q, km_hi) + _dot(q, km_lo)
        blk_lane = lax.broadcasted_iota(jnp.int32, (rows, LANES), 1)
        lane_f = blk_lane.astype(F32)
        gate = jnp.where(blk_lane < n_blocks, gate, -jnp.inf)
        chosen = jnp.zeros((rows, LANES), F32)
        for _k in range(min(MOBA_TOPK, n_blocks)):
            best = jnp.max(gate, axis=1, keepdims=True)
            first = jnp.min(jnp.where(gate == best, lane_f, float(LANES)), axis=1, keepdims=True)
            pick = lane_f == first
            chosen = jnp.where(pick, 1.0, chosen)
            gate = jnp.where(pick, -jnp.inf, gate)
        o_ref[0] = chosen


def _moba_select(q_all, cache_kt, page_table):
    b, n_pages = page_table.shape
    w, page = cache_kt.shape[1:]
    rows = q_all.shape[1]
    steps = n_pages // PAGES_PER_STEP
    n_blocks = n_pages * page // MOBA_BLOCK
    assert n_blocks <= LANES
    grid_spec = pltpu.PrefetchScalarGridSpec(
        num_scalar_prefetch=1,
        grid=(b, steps),
        in_specs=[pl.BlockSpec((1, rows, w), lambda bi, s, pt: (bi, 0, 0))] + _page_specs(n_pages, page, w),
        out_specs=pl.BlockSpec((1, rows, LANES), lambda bi, s, pt: (bi, 0, 0)),
        scratch_shapes=[pltpu.VMEM((w, LANES), F32)],
    )
    return pl.pallas_call(
        functools.partial(_moba_select_body, n_blocks=n_blocks),
        out_shape=jax.ShapeDtypeStruct((b, rows, LANES), F32),
        grid_spec=grid_spec,
        compiler_params=_params(("arbitrary", "arbitrary")),
        name="moba_select",
    )(page_table, q_all, *([cache_kt] * PAGES_PER_STEP))


def _moba_sample_body(pt_ref, q_ref, kn_ref, vn_ref, sel_ref, hm_ref, *rest, t_new):
    k_refs = rest[:PAGES_PER_STEP]
    v_refs = rest[PAGES_PER_STEP:2 * PAGES_PER_STEP]
    o_ref = rest[2 * PAGES_PER_STEP]
    m_ref, l_ref, acc_ref = rest[2 * PAGES_PER_STEP + 1:]
    step = pl.program_id(1)
    q = q_ref[0]
    rows = q.shape[0]
    page = k_refs[0].shape[1]
    lane = lax.broadcasted_iota(jnp.int32, (rows, LANES), 1)

    @pl.when(step == 0)
    def _():
        kn = kn_ref[0].astype(BF16)
        vn = vn_ref[0].astype(BF16)
        qt = lax.broadcasted_iota(jnp.int32, (rows, LANES), 0) % t_new
        ok = lane <= qt
        s = jnp.where(ok, _dot_nt(q, kn), NEG_BIG)
        m = jnp.max(s, axis=1, keepdims=True)
        p = jnp.where(ok, jnp.exp(s - m), 0.0)
        m_ref[...] = jnp.broadcast_to(m, (rows, LANES))
        l_ref[...] = jnp.broadcast_to(jnp.sum(p, axis=1, keepdims=True), (rows, LANES))
        acc_ref[...] = _dot(p.astype(BF16), vn)

    chosen = sel_ref[0]
    pages_per_block = MOBA_BLOCK // page
    scores = []
    for i in range(PAGES_PER_STEP):
        blk = (step * PAGES_PER_STEP + i) // pages_per_block
        picked = jnp.sum(jnp.where(lane == blk, chosen, 0.0), axis=1, keepdims=True)
        ok = jnp.broadcast_to(picked, (rows, page)) > 0.5
        scores.append(jnp.where(ok, _dot(q, k_refs[i][...].astype(BF16)), NEG_BIG))
    m = m_ref[:, 0:1]
    m_new = m
    for s in scores:
        m_new = jnp.maximum(m_new, jnp.max(s, axis=1, keepdims=True))
    alpha = jnp.exp(m - m_new)
    l = alpha * l_ref[:, 0:1]
    acc = alpha * acc_ref[...]
    for i, s in enumerate(scores):
        p = jnp.where(s > 0.5 * NEG_BIG, jnp.exp(s - m_new), 0.0)
        l = l + jnp.sum(p, axis=1, keepdims=True)
        acc = acc + _dot_nt(p.astype(BF16), v_refs[i][...].astype(BF16))
    m_ref[...] = jnp.broadcast_to(m_new, (rows, LANES))
    l_ref[...] = jnp.broadcast_to(l, (rows, LANES))
    acc_ref[...] = acc

    @pl.when(step == pl.num_programs(1) - 1)
    def _():
        o_ref[0] = (acc / l) * hm_ref[...]


def _moba_sample(q_all, head_mask, k_new, v_new, chosen, cache_kt, cache_vt, page_table, t_new):
    b, rows, w = q_all.shape
    n_pages = page_table.shape[1]
    page = cache_kt.shape[2]
    assert page == LANES
    kn = _pad_rows(k_new, LANES)
    vn = _pad_rows(v_new, LANES)
    specs = _page_specs(n_pages, page, w)
    per_b = lambda shp: pl.BlockSpec((1,) + shp, lambda bi, s, pt: (bi, 0, 0))
    const = lambda shp: pl.BlockSpec(shp, lambda bi, s, pt: (0, 0))
    grid_spec = pltpu.PrefetchScalarGridSpec(
        num_scalar_prefetch=1,
        grid=(b, n_pages // PAGES_PER_STEP),
        in_specs=[per_b((rows, w)), per_b((LANES, w)), per_b((LANES, w)), per_b((rows, LANES)),
                  const((rows, w))] + specs + specs,
        out_specs=per_b((rows, w)),
        scratch_shapes=[pltpu.VMEM((rows, LANES), F32), pltpu.VMEM((rows, LANES), F32),
                        pltpu.VMEM((rows, w), F32)],
    )
    o = pl.pallas_call(
        functools.partial(_moba_sample_body, t_new=t_new),
        out_shape=jax.ShapeDtypeStruct((b, rows, w), F32),
        grid_spec=grid_spec,
        compiler_params=_params(("arbitrary", "arbitrary")),
        name="moba_sample",
    )(page_table, q_all, kn, vn, chosen, head_mask, *([cache_kt] * PAGES_PER_STEP),
      *([cache_vt] * PAGES_PER_STEP))
    return _collapse_heads(o, t_new)


def _merge_body(osb_ref, omb_ref, sgsb_ref, sgmb_ref, x_ref, ga_ref, sh_ref, sc_ref, g2_ref, wsb_ref,
                wmb_ref, wout_ref, wr_ref, br_ref, x1_ref, h2_ref, eid_ref, gate_ref):
    u = (sgsb_ref[0].astype(F32) * _dot(osb_ref[0], wsb_ref[...])
         + sgmb_ref[0].astype(F32) * _dot(omb_ref[0], wmb_ref[...]))
    x1 = x_ref[0] + ga_ref[0] * _dot(u.astype(BF16), wout_ref[...])
    x1_ref[0] = x1
    h2 = _modulate(x1, g2_ref[...], sh_ref[0], sc_ref[0])
    for s in range(SUBLANES):
        h2_ref[:, s, :] = h2[:, s * LANES:(s + 1) * LANES]
    logits = _dot3(h2, wr_ref[...]) + br_ref[...]
    rows = logits.shape[0]
    lane = lax.broadcasted_iota(jnp.int32, (rows, LANES), 1)
    lane_f = lane.astype(F32)
    eid = jnp.zeros((rows, LANES), F32)
    top = jnp.zeros((rows, LANES), F32)
    best0 = None
    for k in range(TOP_K):
        best = jnp.max(logits, axis=1, keepdims=True)
        first = jnp.min(jnp.where(logits == best, lane_f, float(LANES)), axis=1, keepdims=True)
        logits = jnp.where(lane_f == first, -jnp.inf, logits)
        if best0 is None:
            best0 = best
        eid = jnp.where(lane == k, first, eid)
        top = jnp.where(lane == k, jnp.exp(best - best0), top)
    eid_ref[0] = eid.astype(jnp.int32)
    gate_ref[0] = top / jnp.sum(top, axis=1, keepdims=True)


def _merge(o_sb, o_mb, sg_sb, sg_mb, x, ga1, sh2, sc2, g2, w_sb16, w_mb16, w_out16, w_router_pad,
           b_router_pad, tm):
    b, t, d = x.shape
    assert d == SUBLANES * LANES
    r = ga1.shape[1]
    nt = t // tm
    tile = lambda wd: pl.BlockSpec((1, tm, wd), lambda bi, ti: (bi, ti, 0))
    const = lambda shp: pl.BlockSpec(shp, lambda bi, ti: (0,) * len(shp))
    mod = pl.BlockSpec((1, r, d), lambda bi, ti: (bi, 0, 0))
    return pl.pallas_call(
        _merge_body,
        out_shape=[jax.ShapeDtypeStruct((b, t, d), F32), jax.ShapeDtypeStruct((b * t, SUBLANES, LANES), F32),
                   jax.ShapeDtypeStruct((b, t, LANES), jnp.int32), jax.ShapeDtypeStruct((b, t, LANES), F32)],
        grid=(b, nt),
        in_specs=[tile(WIDTH), tile(WIDTH), tile(d), tile(d), tile(d), mod, mod, mod, const((1, d)),
                  const((WIDTH, d)), const((WIDTH, d)), const((d, d)), const((d, LANES)), const((1, LANES))],
        out_specs=[tile(d), pl.BlockSpec((tm, SUBLANES, LANES), lambda bi, ti: (bi * nt + ti, 0, 0)),
                   tile(LANES), tile(LANES)],
        compiler_params=_params(("arbitrary", "arbitrary")),
        name="merge",
    )(o_sb, o_mb, sg_sb, sg_mb, x, ga1, sh2, sc2, g2, w_sb16, w_mb16, w_out16, w_router_pad, b_router_pad)


def _ffn_body(be_ref, nu_ref, tok_ref, tokn_ref, dst_ref, h2_hbm, wgu_ref, bgu_ref, wd_ref, bd_ref, y_hbm,
              xbuf, obuf, wgu16, wd16, gsem, ssem, *, tm, ff):
    j = pl.program_id(0)
    n_used = nu_ref[0]
    cur = j % 2

    def start_gather(idx_ref, buf):
        def body(r, c):
            pltpu.make_async_copy(h2_hbm.at[idx_ref[0, 0, r]], xbuf.at[buf, r], gsem.at[buf]).start()
            return c
        lax.fori_loop(0, tm, body, 0, unroll=8)

    def wait_gather(buf):
        pltpu.make_async_copy(h2_hbm.at[pl.ds(0, tm)], xbuf.at[buf], gsem.at[buf]).wait()

    def start_scatter(buf):
        def body(r, c):
            pltpu.make_async_copy(obuf.at[buf, r], y_hbm.at[dst_ref[0, 0, r]], ssem.at[buf]).start()
            return c
        lax.fori_loop(0, tm, body, 0, unroll=8)

    def wait_scatter(buf):
        pltpu.make_async_copy(obuf.at[buf], y_hbm.at[pl.ds(0, tm)], ssem.at[buf]).wait()

    @pl.when(j == 0)
    def _():
        n_slots = y_hbm.shape[0] - 2 * tm
        obuf[0] = jnp.zeros(obuf.shape[1:], F32)
        for half in range(2):
            fill = pltpu.make_async_copy(obuf.at[0], y_hbm.at[pl.ds(n_slots + half * tm, tm)], ssem.at[0])
            fill.start()
            fill.wait()

    @pl.when(j < n_used)
    def _():
        @pl.when(j == 0)
        def _():
            start_gather(tok_ref, 0)

        @pl.when(j + 1 < n_used)
        def _():
            start_gather(tokn_ref, 1 - cur)

        @pl.when((j == 0) | (be_ref[j] != be_ref[jnp.maximum(j - 1, 0)]))
        def _():
            wgu16[...] = wgu_ref[0].astype(BF16)
            wd16[...] = wd_ref[0].astype(BF16)

        wait_gather(cur)
        x16 = jnp.concatenate([xbuf[cur, :, s, :].astype(BF16) for s in range(SUBLANES)], axis=1)
        y = jnp.zeros((tm, wd16.shape[1]), F32) + bd_ref[0]
        chunk = 512
        for c in range(ff // chunk):
            g = _dot(x16, wgu16[:, c * chunk:(c + 1) * chunk]) + bgu_ref[0, :, c * chunk:(c + 1) * chunk]
            u = (_dot(x16, wgu16[:, ff + c * chunk:ff + (c + 1) * chunk])
                 + bgu_ref[0, :, ff + c * chunk:ff + (c + 1) * chunk])
            g = jnp.minimum(g, SWIGLU_LIMIT)
            u = jnp.clip(u, -SWIGLU_LIMIT, SWIGLU_LIMIT)
            act = (u + 1.0) * (g * jax.nn.sigmoid(SWIGLU_ALPHA * g))
            y = y + _dot(act.astype(BF16), wd16[c * chunk:(c + 1) * chunk, :])

        @pl.when(j >= 2)
        def _():
            wait_scatter(cur)

        for s in range(SUBLANES):
            obuf[cur, :, s, :] = y[:, s * LANES:(s + 1) * LANES]
        start_scatter(cur)

        @pl.when(j == n_used - 1)
        def _():
            wait_scatter(cur)

            @pl.when(j >= 1)
            def _():
                wait_scatter(1 - cur)


def _expert_ffn(h2_rows, row_tok, row_dst, block_e, n_used, w_gu, b_gu, w_down, b_down, n_out_rows):
    n_blocks, _, tm = row_tok.shape
    e, d, ff2 = w_gu.shape
    ff = ff2 // 2

    def blk(j, be, nu):
        return (jnp.minimum(j, nu[0] - 1), 0, 0)

    def blk_next(j, be, nu):
        return (jnp.minimum(j + 1, nu[0] - 1), 0, 0)

    smem = lambda index: pl.BlockSpec((1, 1, tm), index, memory_space=pltpu.SMEM)
    per_e = lambda shp: pl.BlockSpec((1,) + shp, lambda j, be, nu: (be[j], 0, 0))
    grid_spec = pltpu.PrefetchScalarGridSpec(
        num_scalar_prefetch=2,
        grid=(n_blocks,),
        in_specs=[smem(blk), smem(blk_next), smem(blk), pl.BlockSpec(memory_space=pl.ANY),
                  per_e((d, ff2)), per_e((1, ff2)), per_e((ff, d)), per_e((1, d))],
        out_specs=pl.BlockSpec(memory_space=pl.ANY),
        scratch_shapes=[pltpu.VMEM((2, tm, SUBLANES, LANES), F32), pltpu.VMEM((2, tm, SUBLANES, LANES), F32),
                        pltpu.VMEM((d, ff2), BF16), pltpu.VMEM((ff, d), BF16),
                        pltpu.SemaphoreType.DMA((2,)), pltpu.SemaphoreType.DMA((2,))],
    )
    return pl.pallas_call(
        functools.partial(_ffn_body, tm=tm, ff=ff),
        out_shape=jax.ShapeDtypeStruct((n_out_rows, SUBLANES, LANES), F32),
        grid_spec=grid_spec,
        compiler_params=_params(("arbitrary",), vmem=56 * 1024 * 1024),
        name="expert_ffn",
    )(block_e, n_used, row_tok, row_tok, row_dst, h2_rows, w_gu, b_gu.reshape(e, 1, ff2), w_down,
      b_down.reshape(e, 1, d))


def _route(eid, n_tokens):
    tile = EXPERT_TILE
    nk = n_tokens * TOP_K
    n_blocks = (nk + N_EXPERTS * (tile - 1) + tile - 1) // tile
    rows = n_blocks * tile
    onehot = (eid[:, :, None] == jnp.arange(N_EXPERTS, dtype=jnp.int32)[None, None, :]).astype(jnp.int32)
    per_token = onehot.sum(axis=1)
    before = jnp.cumsum(per_token, axis=0) - per_token
    counts = per_token.sum(axis=0)
    padded = (counts + tile - 1) // tile * tile
    pend = jnp.cumsum(padded)
    pstart = pend - padded
    dest = ((before + pstart[None, :])[:, None, :] * onehot).sum(axis=2).reshape(-1).astype(jnp.int32)
    pos = jnp.arange(rows, dtype=jnp.int32)
    row_flat = jnp.full((rows,), -1, jnp.int32).at[dest].set(
        jnp.arange(nk, dtype=jnp.int32), unique_indices=True)
    is_pad = row_flat < 0
    dump = nk + ((pos // tile) % 2) * tile + pos % tile
    row_tok = jnp.where(is_pad, 0, row_flat // TOP_K).astype(jnp.int32)
    row_dst = jnp.where(is_pad, dump, (row_flat % TOP_K) * n_tokens + row_flat // TOP_K).astype(jnp.int32)
    n_used = (pend[-1] // tile).astype(jnp.int32)
    blk_start = jnp.minimum(jnp.arange(n_blocks), n_used - 1) * tile
    block_e = jnp.minimum((pend[None, :] <= blk_start[:, None]).sum(axis=1), N_EXPERTS - 1).astype(jnp.int32)
    return (row_tok.reshape(n_blocks, 1, tile), row_dst.reshape(n_blocks, 1, tile), block_e,
            n_used.reshape(1), nk + 2 * tile)


def _combine_body(y0_ref, y1_ref, y2_ref, y3_ref, gate_ref, x1_ref, ga_ref, o_ref):
    gate = gate_ref[0]
    y_refs = (y0_ref, y1_ref, y2_ref, y3_ref)
    for s in range(SUBLANES):
        ffn = None
        for k in range(TOP_K):
            part = gate[:, k:k + 1] * y_refs[k][:, s, :]
            ffn = part if ffn is None else ffn + part
        lanes = slice(s * LANES, (s + 1) * LANES)
        o_ref[0, :, lanes] = x1_ref[0, :, lanes] + ga_ref[0, :, lanes] * ffn


def _combine(y, gate, x1, ga2, tm, first_token, n_tokens):
    b, t, d = x1.shape
    r = ga2.shape[1]
    nt = t // tm
    assert first_token % tm == 0 and n_tokens % tm == 0
    tile = lambda wd: pl.BlockSpec((1, tm, wd), lambda bi, ti: (bi, ti, 0))

    def y_spec(k):
        base = (k * n_tokens + first_token) // tm
        return pl.BlockSpec((tm, SUBLANES, LANES), lambda bi, ti: (base + bi * nt + ti, 0, 0))

    return pl.pallas_call(
        _combine_body,
        out_shape=jax.ShapeDtypeStruct((b, t, d), F32),
        grid=(b, nt),
        in_specs=[y_spec(k) for k in range(TOP_K)]
        + [tile(LANES), tile(d), pl.BlockSpec((1, r if r == 1 else tm, d),
                                              lambda bi, ti: (bi, 0, 0) if r == 1 else (bi, ti, 0))],
        out_specs=tile(d),
        compiler_params=_params(("arbitrary", "arbitrary")),
        name="combine",
    )(y, y, y, y, gate, x1, ga2)


def _head_gain(g):
    return jnp.tile(g.astype(F32), N_HEADS).reshape(1, WIDTH)


def _mods(mod, d):
    return [mod[:, None, i * d:(i + 1) * d] for i in range(N_MOD)]


def kernel(x_prompt, x_sample, c_prompt, c_sample, cache_sb_k, cache_sb_v, cache_mb_k, cache_mb_v, page_table,
           w_ada, b_ada, g_norm1, g_norm2, w_in, q_norm_g, k_norm_g, w_br_sb, w_br_mb, w_out, w_router,
           b_router, w_gu, b_gu, w_down, b_down):
    depth = w_ada.shape[0]
    assert depth == 1
    bp, seq, d = x_prompt.shape
    bs, t_new, _ = x_sample.shape
    page = cache_sb_k.shape[2]
    past = page_table.shape[1] * page
    n_prompt = bp * seq
    n_sample = bs * t_new
    lyr = 0

    mod = _adaln(jnp.concatenate([c_prompt, c_sample], axis=0), w_ada[lyr], b_ada[lyr])
    sh1p, sc1p, ga1p, sh2p, sc2p, ga2p = _mods(mod[:bp], d)
    per_row = lambda m: jnp.repeat(m, t_new, axis=0).reshape(1, n_sample, d)
    sh1s, sc1s, ga1s, sh2s, sc2s, ga2s = [per_row(m[:, 0]) for m in _mods(mod[bp:], d)]

    w_in16 = w_in[lyr].astype(BF16)
    g1 = g_norm1[lyr].reshape(1, d)
    g2 = g_norm2[lyr].reshape(1, d)
    qg = _head_gain(q_norm_g[lyr])
    kg = _head_gain(k_norm_g[lyr])
    head_of = np.arange(WIDTH) // HEAD_DIM
    gmat = jnp.asarray((head_of[:, None] == head_of[None, :]).astype(np.float32) / HEAD_DIM, dtype=BF16)

    pos_p = jnp.arange(seq, dtype=jnp.int32)
    (qsb, ksb, vsb, ksb16, vsb16, qmb, kmb, vmb, kmb16, vmb16, sgsb, sgmb, kmean) = _inproj(
        x_prompt, sh1p, sc1p, g1, w_in16, pos_p, qg, kg, gmat, ROW_TILE, True)
    o_sb_p = _sb_prompt(qsb, ksb16, vsb16)
    o_mb_p = _moba_prompt(qmb, kmb16, vmb16, kmean)

    xs = x_sample.reshape(1, n_sample, d)
    pos_s = past + jnp.arange(n_sample, dtype=jnp.int32) % t_new
    (qsb_s, ksb_s, vsb_s, _, _, qmb_s, kmb_s, vmb_s, _, _, sgsb_s, sgmb_s) = _inproj(
        xs, sh1s, sc1s, g1, w_in16, pos_s, qg, kg, gmat, n_sample, False)
    by_batch = lambda a: a.reshape(bs, t_new, WIDTH)
    o_sb_s = _sb_sample(by_batch(qsb_s), by_batch(ksb_s), by_batch(vsb_s), _pages_t(cache_sb_k, lyr),
                        _pages_t(cache_sb_v, lyr), page_table)
    mb_kt = _pages_t(cache_mb_k, lyr)
    q_all_mb, head_mask = _expand_heads(by_batch(qmb_s))
    chosen = _moba_select(q_all_mb, mb_kt, page_table)
    o_mb_s = _moba_sample(q_all_mb, head_mask, by_batch(kmb_s), by_batch(vmb_s), chosen, mb_kt,
                          _pages_t(cache_mb_v, lyr), page_table, t_new)
    o_sb_s = o_sb_s.reshape(1, n_sample, WIDTH).astype(BF16)
    o_mb_s = o_mb_s.reshape(1, n_sample, WIDTH).astype(BF16)

    w_sb16 = w_br_sb[lyr].astype(BF16)
    w_mb16 = w_br_mb[lyr].astype(BF16)
    w_out16 = w_out[lyr].astype(BF16)
    w_router_pad = jnp.pad(w_router[lyr], ((0, 0), (0, LANES - N_EXPERTS)))
    b_router_pad = jnp.concatenate([b_router[lyr].astype(F32),
                                    jnp.full((LANES - N_EXPERTS,), -jnp.inf, F32)]).reshape(1, LANES)
    x1p, h2p, eidp, gatep = _merge(o_sb_p, o_mb_p, sgsb, sgmb, x_prompt, ga1p, sh2p, sc2p, g2, w_sb16,
                                   w_mb16, w_out16, w_router_pad, b_router_pad, ROW_TILE)
    x1s, h2s, eids, gates = _merge(o_sb_s, o_mb_s, sgsb_s, sgmb_s, xs, ga1s, sh2s, sc2s, g2, w_sb16,
                                   w_mb16, w_out16, w_router_pad, b_router_pad, n_sample)

    n_tokens = n_prompt + n_sample
    h2_rows = jnp.concatenate([h2p, h2s])
    eid = jnp.concatenate([eidp.reshape(n_prompt, LANES), eids.reshape(n_sample, LANES)])[:, :TOP_K]
    row_tok, row_dst, block_e, n_used, n_out_rows = _route(eid, n_tokens)
    y_rows = _expert_ffn(h2_rows, row_tok, row_dst, block_e, n_used, w_gu[lyr], b_gu[lyr], w_down[lyr],
                         b_down[lyr], n_out_rows)
    y_prompt = _combine(y_rows, gatep, x1p, ga2p, COMBINE_TILE, 0, n_tokens)
    y_sample = _combine(y_rows, gates, x1s, ga2s, COMBINE_TILE, n_prompt, n_tokens)

    heads = lambda a, b_, t_: a.reshape(1, b_, t_, N_HEADS, HEAD_DIM)
    return (y_prompt, y_sample.reshape(bs, t_new, d),
            heads(ksb, bp, seq), heads(vsb, bp, seq), heads(kmb, bp, seq), heads(vmb, bp, seq),
            heads(ksb_s, bs, t_new), heads(vsb_s, bs, t_new), heads(kmb_s, bs, t_new), heads(vmb_s, bs, t_new))
```

```python
import functools

import numpy as np
import jax
import jax.numpy as jnp
from jax import lax
from jax.experimental import pallas as pl
from jax.experimental.pallas import tpu as pltpu

HEAD_DIM = 64
N_HEADS = 8
WIDTH = N_HEADS * HEAD_DIM
MOBA_BLOCK = 256
MOBA_TOPK = 3
ROPE_THETA = 10000.0
N_EXPERTS = 32
TOP_K = 4
SWIGLU_LIMIT = 7.0
SWIGLU_ALPHA = 1.702
NORM_EPS = 1e-6
N_MOD = 6
ATTN_SCALE = HEAD_DIM ** -0.5

LANES = 128
SUBLANES = 8
ROW_TILE = 256
ATTN_TILE = 256
EXPERT_TILE = 512
COMBINE_TILE = 128
PAGES_PER_STEP = 8
NEG_BIG = -1e30
STICK_FLOOR = -104.0
VMEM_LIMIT = 48 * 1024 * 1024

F32 = jnp.float32
BF16 = jnp.bfloat16


def _dot(a, b):
    return jnp.dot(a, b, preferred_element_type=F32)


def _dot_nt(a, b):
    return lax.dot_general(a, b, (((1,), (1,)), ((), ())), preferred_element_type=F32)


def _split(x):
    hi = x.astype(BF16)
    lo = (x - hi.astype(F32)).astype(BF16)
    return hi, lo


def _dot3(a, b):
    a_hi, a_lo = _split(a)
    b_hi, b_lo = _split(b)
    return _dot(a_hi, b_hi) + _dot(a_lo, b_hi) + _dot(a_hi, b_lo)


def _params(sem, vmem=VMEM_LIMIT):
    return pltpu.CompilerParams(dimension_semantics=sem, vmem_limit_bytes=vmem)


def _adaln_body(c_ref, w_ref, b_ref, o_ref):
    c = c_ref[...]
    s = c * jax.nn.sigmoid(c)
    o_ref[...] = _dot3(s, w_ref[...]) + b_ref[...]


def _adaln(c, w_ada, b_ada):
    rows, d = c.shape
    cols = w_ada.shape[1]
    tn = 1024
    return pl.pallas_call(
        _adaln_body,
        out_shape=jax.ShapeDtypeStruct((rows, cols), F32),
        grid=(cols // tn,),
        in_specs=[pl.BlockSpec((rows, d), lambda j: (0, 0)),
                  pl.BlockSpec((d, tn), lambda j: (0, j)),
                  pl.BlockSpec((1, tn), lambda j: (0, j))],
        out_specs=pl.BlockSpec((rows, tn), lambda j: (0, j)),
        compiler_params=_params(("arbitrary",)),
        name="adaln",
    )(c, w_ada, b_ada.reshape(1, cols))


def _modulate(x, g, shift, scale):
    ms = jnp.mean(x * x, axis=-1, keepdims=True)
    return (x * lax.rsqrt(ms + NORM_EPS) * g) * (1.0 + scale) + shift


def _qk_norm_rope(p, g, gmat, cos, sin_signed):
    ms = _dot((p * p).astype(BF16), gmat)
    y = p * lax.rsqrt(ms + NORM_EPS) * g
    lane = lax.broadcasted_iota(jnp.int32, (p.shape[0], LANES), 1)
    first_half = (lane & (HEAD_DIM // 2)) == 0
    outs = []
    for c in range(WIDTH // LANES):
        yc = y[:, c * LANES:(c + 1) * LANES]
        partner = jnp.where(first_half, pltpu.roll(yc, LANES - HEAD_DIM // 2, 1),
                            pltpu.roll(yc, HEAD_DIM // 2, 1))
        outs.append(yc * cos + partner * sin_signed)
    return jnp.concatenate(outs, axis=1)


def _inproj_body(emit_kmean, x_ref, sh_ref, sc_ref, g1_ref, w_ref, cos_ref, sin_ref, qg_ref, kg_ref,
                 gmat_ref, qsb_ref, ksb_ref, vsb_ref, ksb16_ref, vsb16_ref, qmb_ref, kmb_ref, vmb_ref,
                 kmb16_ref, vmb16_ref, sgsb_ref, sgmb_ref, *rest):
    h = _modulate(x_ref[0], g1_ref[...], sh_ref[0], sc_ref[0])
    h16 = h.astype(BF16)

    def proj(c0, width):
        return _dot(h16, w_ref[:, c0:c0 + width])

    w = WIDTH
    qsb_ref[0] = (proj(0, w) * ATTN_SCALE).astype(BF16)
    ksb = proj(w, w)
    ksb_ref[0] = ksb
    ksb16_ref[0] = ksb.astype(BF16)
    vsb = proj(2 * w, w)
    vsb_ref[0] = vsb
    vsb16_ref[0] = vsb.astype(BF16)
    cos = cos_ref[...]
    sin = sin_ref[...]
    gmat = gmat_ref[...]
    qmb = _qk_norm_rope(proj(3 * w, w), qg_ref[...], gmat, cos, sin)
    qmb_ref[0] = (qmb * ATTN_SCALE).astype(BF16)
    kmb = _qk_norm_rope(proj(4 * w, w), kg_ref[...], gmat, cos, sin)
    kmb_ref[0] = kmb
    kmb16_ref[0] = kmb.astype(BF16)
    vmb = proj(5 * w, w)
    vmb_ref[0] = vmb
    vmb16_ref[0] = vmb.astype(BF16)
    d = x_ref.shape[2]
    sgsb_ref[0] = jax.nn.sigmoid(proj(6 * w, d)).astype(BF16)
    sgmb_ref[0] = jax.nn.sigmoid(proj(6 * w + d, d)).astype(BF16)
    if emit_kmean:
        kmean_ref, vmbt_ref = rest
        ti = pl.program_id(1)
        kmean_ref[0, pl.ds(ti, 1), :] = jnp.mean(kmb, axis=0, keepdims=True)
        vmbt_ref[0, 0] = vmb.T.astype(BF16)


def _rope_tables(pos):
    inv_freq = ROPE_THETA ** (-jnp.arange(0, HEAD_DIM, 2, dtype=F32) / HEAD_DIM)
    ang = pos.astype(F32)[:, None] * inv_freq[None, :]
    cos = jnp.tile(jnp.cos(ang), (1, LANES // (HEAD_DIM // 2)))
    sin = jnp.sin(ang)
    sin_signed = jnp.tile(jnp.concatenate([-sin, sin], axis=1), (1, LANES // HEAD_DIM))
    return cos, sin_signed


def _inproj(x, shift, scale, g1, w_in16, pos, qg, kg, gmat, tm, emit_kmean):
    b, t, d = x.shape
    r = shift.shape[1]
    cols = w_in16.shape[1]
    nt = t // tm
    cos, sin_signed = _rope_tables(pos)
    row = lambda dt, wd: jax.ShapeDtypeStruct((b, t, wd), dt)
    tile = lambda wd: pl.BlockSpec((1, tm, wd), lambda bi, ti: (bi, ti, 0))
    const = lambda shp: pl.BlockSpec(shp, lambda bi, ti: (0,) * len(shp))
    mod = pl.BlockSpec((1, r, d), lambda bi, ti: (bi, 0, 0))
    out_shape = [row(BF16, WIDTH), row(F32, WIDTH), row(F32, WIDTH), row(BF16, WIDTH), row(BF16, WIDTH),
                 row(BF16, WIDTH), row(F32, WIDTH), row(F32, WIDTH), row(BF16, WIDTH), row(BF16, WIDTH),
                 row(BF16, d), row(BF16, d)]
    out_specs = [tile(WIDTH)] * 10 + [tile(d)] * 2
    if emit_kmean:
        assert tm == MOBA_BLOCK
        out_shape.append(jax.ShapeDtypeStruct((b, nt, WIDTH), F32))
        out_specs.append(pl.BlockSpec((1, nt, WIDTH), lambda bi, ti: (bi, 0, 0)))
        out_shape.append(jax.ShapeDtypeStruct((b, nt, WIDTH, tm), BF16))
        out_specs.append(pl.BlockSpec((1, 1, WIDTH, tm), lambda bi, ti: (bi, ti, 0, 0)))
    return pl.pallas_call(
        functools.partial(_inproj_body, emit_kmean),
        out_shape=out_shape,
        grid=(b, nt),
        in_specs=[tile(d), mod, mod, const((1, d)), const((d, cols)),
                  pl.BlockSpec((tm, LANES), lambda bi, ti: (ti, 0)),
                  pl.BlockSpec((tm, LANES), lambda bi, ti: (ti, 0)),
                  const((1, WIDTH)), const((1, WIDTH)), const((WIDTH, WIDTH))],
        out_specs=out_specs,
        compiler_params=_params(("arbitrary", "arbitrary")),
        name="inproj_kmean" if emit_kmean else "inproj",
    )(x, shift, scale, g1, w_in16, cos, sin_signed, qg, kg, gmat)


def _stick_tile(z, allowed, tri, carry):
    sp = jnp.maximum(z, 0.0) + jnp.log(1.0 + jnp.exp(-jnp.abs(z)))
    log_keep = -sp
    if allowed is not None:
        log_keep = jnp.where(allowed, log_keep, 0.0)
    lk_hi, lk_lo = _split(log_keep)
    later = _dot(lk_hi, tri) + _dot(lk_lo, tri)
    ones = jnp.ones((z.shape[1], LANES), BF16)
    total = _dot(lk_hi, ones) + _dot(lk_lo, ones)
    reps = z.shape[1] // LANES
    carry_wide = carry if reps == 1 else jnp.concatenate([carry] * reps, axis=1)
    a = jnp.exp((z - sp) + later + carry_wide)
    if allowed is not None:
        a = jnp.where(allowed, a, 0.0)
    return a, carry + total


def _head_masks(rows):
    lane = lax.broadcasted_iota(jnp.int32, (rows, LANES), 1)
    return lane < HEAD_DIM


def _sb_prompt_body(q_ref, k_ref, v_ref, tri_ref, o_ref, *, tile):
    qi = pl.program_id(2)
    q = q_ref[0]
    tri = tri_ref[...]
    low = _head_masks(tile)
    row = lax.broadcasted_iota(jnp.int32, (tile, tile), 0)
    col = lax.broadcasted_iota(jnp.int32, (tile, tile), 1)
    diag_allowed = col < row
    zeros16 = jnp.zeros_like(q)
    qms = (jnp.where(low, q, zeros16), jnp.where(low, zeros16, q))

    def key_tile(kb, state, allowed):
        start = pl.multiple_of(kb * tile, tile)
        k_blk = k_ref[0, pl.ds(start, tile), :]
        v_blk = v_ref[0, pl.ds(start, tile), :]
        out = []
        for h in range(2):
            carry, acc = state[h]
            a, carry = _stick_tile(_dot_nt(qms[h], k_blk), allowed, tri, carry)
            out.append((carry, acc + _dot(a.astype(BF16), v_blk)))
        return tuple(out)

    zero = jnp.zeros((tile, LANES), F32)
    state = key_tile(qi, ((zero, zero), (zero, zero)), diag_allowed)

    def live(st):
        return jnp.maximum(jnp.max(st[0][0]), jnp.max(st[1][0])) > STICK_FLOOR

    def cond(c):
        return (c[0] < qi) & live(c[1])

    def body(c):
        return c[0] + 1, key_tile(qi - 1 - c[0], c[1], None)

    _, state = lax.while_loop(cond, body, (jnp.int32(0), state))
    o_ref[0] = jnp.where(low, state[0][1], state[1][1]).astype(o_ref.dtype)


def _tri(n):
    j = np.arange(n)[:, None]
    s = np.arange(n)[None, :]
    return jnp.asarray((j > s).astype(np.float32), dtype=BF16)


def _sb_prompt(q16, k16, v16):
    b, s, w = q16.shape
    tile = ATTN_TILE
    pairs = w // LANES
    return pl.pallas_call(
        functools.partial(_sb_prompt_body, tile=tile),
        out_shape=jax.ShapeDtypeStruct((b, s, w), BF16),
        grid=(b, pairs, s // tile),
        in_specs=[pl.BlockSpec((1, tile, LANES), lambda bi, p, qi: (bi, qi, p)),
                  pl.BlockSpec((1, s, LANES), lambda bi, p, qi: (bi, 0, p)),
                  pl.BlockSpec((1, s, LANES), lambda bi, p, qi: (bi, 0, p)),
                  pl.BlockSpec((tile, tile), lambda bi, p, qi: (0, 0))],
        out_specs=pl.BlockSpec((1, tile, LANES), lambda bi, p, qi: (bi, qi, p)),
        compiler_params=_params(("arbitrary",) * 3),
        name="sb_prompt",
    )(q16, k16, v16, _tri(tile))


def _moba_prompt_body(q_ref, k_ref, vt_ref, km_ref, o_ref, sel_ref, *, tile, nb):
    own = pl.program_id(2)
    q = q_ref[0]
    assert nb <= SUBLANES
    km_pad = jnp.concatenate([km_ref[0], jnp.zeros((2 * SUBLANES - nb, LANES), F32)], axis=0)
    km_hi, km_lo = _split(km_pad)
    low = _head_masks(tile)
    zeros16 = jnp.zeros_like(q)
    qms = (jnp.where(low, q, zeros16), jnp.where(low, zeros16, q))
    blk = lax.broadcasted_iota(jnp.int32, (SUBLANES, tile), 0)
    past = blk < own
    for h in range(2):
        gate = (_dot_nt(km_hi, qms[h]) + _dot_nt(km_lo, qms[h]))[:SUBLANES]
        gate = jnp.where(past, gate, -jnp.inf)
        ahead = jnp.zeros((SUBLANES, tile), F32)
        for r in range(1, SUBLANES):
            other = pltpu.roll(gate, r, 0)
            other_blk = (blk - r) & (SUBLANES - 1)
            ahead = ahead + jnp.where(other > gate, 1.0,
                                      jnp.where((other == gate) & (other_blk < blk), 1.0, 0.0))
        sel_ref[h] = jnp.where(past, jnp.where(ahead < MOBA_TOPK, 1.0, 0.0), 0.0)

    key_i = lax.broadcasted_iota(jnp.int32, (tile, tile), 0)
    qry_i = lax.broadcasted_iota(jnp.int32, (tile, tile), 1)
    causal = key_i <= qry_i

    def key_tile(kb, state, allowed):
        start = pl.multiple_of(kb * tile, tile)
        k_blk = k_ref[0, pl.ds(start, tile), :]
        vt_blk = vt_ref[0, kb]
        out = []
        for h in range(2):
            m, l, acc = state[h]
            ok = allowed
            if ok is None:
                ok = jnp.broadcast_to(sel_ref[h, pl.ds(kb, 1), :], (tile, tile)) > 0.5
            s = jnp.where(ok, _dot_nt(k_blk, qms[h]), NEG_BIG)
            m_new = jnp.maximum(m, jnp.max(s, axis=0, keepdims=True))
            p = jnp.where(ok, jnp.exp(s - m_new), 0.0)
            alpha = jnp.exp(m - m_new)
            l = alpha * l + jnp.sum(p, axis=0, keepdims=True)
            acc = alpha * acc + _dot(vt_blk, p.astype(BF16))
            out.append((m_new, l, acc))
        return tuple(out)

    m0 = jnp.full((1, tile), NEG_BIG, F32)
    l0 = jnp.zeros((1, tile), F32)
    a0 = jnp.zeros((LANES, tile), F32)
    state = key_tile(own, ((m0, l0, a0), (m0, l0, a0)), causal)
    state = lax.fori_loop(0, own, lambda kb, st: key_tile(kb, st, None), state)
    outs = [state[h][2] / state[h][1] for h in range(2)]
    dim = lax.broadcasted_iota(jnp.int32, (LANES, tile), 0)
    o_ref[0] = jnp.where(dim < HEAD_DIM, outs[0], outs[1]).T.astype(o_ref.dtype)


def _moba_prompt(q16, k16, vt16, kmean):
    b, s, w = q16.shape
    tile = MOBA_BLOCK
    nb = s // tile
    pairs = w // LANES
    return pl.pallas_call(
        functools.partial(_moba_prompt_body, tile=tile, nb=nb),
        out_shape=jax.ShapeDtypeStruct((b, s, w), BF16),
        grid=(b, pairs, nb),
        in_specs=[pl.BlockSpec((1, tile, LANES), lambda bi, p, qi: (bi, qi, p)),
                  pl.BlockSpec((1, s, LANES), lambda bi, p, qi: (bi, 0, p)),
                  pl.BlockSpec((1, nb, LANES, tile), lambda bi, p, qi: (bi, 0, p, 0)),
                  pl.BlockSpec((1, nb, LANES), lambda bi, p, qi: (bi, 0, p))],
        out_specs=pl.BlockSpec((1, tile, LANES), lambda bi, p, qi: (bi, qi, p)),
        scratch_shapes=[pltpu.VMEM((2, SUBLANES, tile), F32)],
        compiler_params=_params(("arbitrary",) * 3),
        name="moba_prompt",
    )(q16, k16, vt16, kmean)


def _pages_t(cache, layer):
    n_phys, page = cache.shape[1:3]
    return jnp.transpose(cache[layer], (0, 2, 3, 1)).reshape(n_phys, WIDTH, page)


def _page_specs(n_pages, page, width):
    specs = []
    for i in range(PAGES_PER_STEP):
        def index(bi, s, pt, i=i):
            return (pt[bi, s * PAGES_PER_STEP + i], 0, 0)
        specs.append(pl.BlockSpec((None, width, page), index))
    return specs


def _expand_heads(q):
    b, t, w = q.shape
    head_of_lane = jnp.arange(w) // HEAD_DIM
    mask = (head_of_lane[None, :] == jnp.arange(N_HEADS)[:, None]).astype(q.dtype)
    return (q[:, None, :, :] * mask[None, :, None, :]).reshape(b, N_HEADS * t, w), \
        jnp.repeat(mask, t, axis=0).astype(F32)


def _collapse_heads(o, t):
    b, _, w = o.shape
    return o.reshape(b, N_HEADS, t, w).sum(axis=1)


def _pad_rows(x, rows):
    return jnp.pad(x, ((0, 0), (0, rows - x.shape[1]), (0, 0)))


def _sb_sample_body(pt_ref, q_ref, kn_ref, vn_ref, tri_ref, hm_ref, kc_hbm, vc_hbm, o_ref, kbuf, vbuf, sem,
                    *, n_pages, t_new):
    b = pl.program_id(0)
    q = q_ref[0]
    rows = q.shape[0]
    tri = tri_ref[...]

    def page_copies(bi, p, slot):
        phys = pt_ref[bi, p]
        return (pltpu.make_async_copy(kc_hbm.at[phys], kbuf.at[slot], sem.at[0, slot]),
                pltpu.make_async_copy(vc_hbm.at[phys], vbuf.at[slot], sem.at[1, slot]))

    def start(bi, p, slot):
        for c in page_copies(bi, p, slot):
            c.start()

    def wait(bi, p, slot):
        for c in page_copies(bi, p, slot):
            c.wait()

    def slot_of(p):
        return (n_pages - 1 - p) % 2

    @pl.when(b == 0)
    def _():
        start(b, n_pages - 1, 0)

    kn = kn_ref[0].astype(BF16)
    vn = vn_ref[0].astype(BF16)
    key = lax.broadcasted_iota(jnp.int32, (rows, LANES), 1)
    qt = lax.broadcasted_iota(jnp.int32, (rows, LANES), 0) % t_new
    a, carry = _stick_tile(_dot_nt(q, kn), key < qt, tri, jnp.zeros((rows, LANES), F32))
    acc = _dot(a.astype(BF16), vn)

    def cond(c):
        return (c[0] >= 0) & (jnp.max(c[1]) > STICK_FLOOR)

    def body(c):
        p, carry, acc = c
        slot = slot_of(p)
        wait(b, p, slot)

        @pl.when(p >= 1)
        def _():
            start(b, p - 1, 1 - slot)

        k_t = kbuf[slot].astype(BF16)
        v_t = vbuf[slot].astype(BF16)
        a, carry = _stick_tile(_dot(q, k_t), None, tri, carry)
        return p - 1, carry, acc + _dot_nt(a.astype(BF16), v_t)

    p, carry, acc = lax.while_loop(cond, body, (jnp.int32(n_pages - 1), carry, acc))

    @pl.when(p >= 0)
    def _():
        wait(b, p, slot_of(p))

    @pl.when(b + 1 < pl.num_programs(0))
    def _():
        start(b + 1, n_pages - 1, 0)

    o_ref[0] = acc * hm_ref[...]


def _sb_sample(q16, k_new, v_new, cache_kt, cache_vt, page_table):
    b, t, w = q16.shape
    n_pages = page_table.shape[1]
    page = cache_kt.shape[2]
    q_all, head_mask = _expand_heads(q16)
    rows = q_all.shape[1]
    assert page == LANES
    kn = _pad_rows(k_new, LANES)
    vn = _pad_rows(v_new, LANES)
    per_b = lambda shp: pl.BlockSpec((1,) + shp, lambda bi, pt: (bi, 0, 0))
    const = lambda shp: pl.BlockSpec(shp, lambda bi, pt: (0, 0))
    grid_spec = pltpu.PrefetchScalarGridSpec(
        num_scalar_prefetch=1,
        grid=(b,),
        in_specs=[per_b((rows, w)), per_b((LANES, w)), per_b((LANES, w)), const((page, page)),
                  const((rows, w)), pl.BlockSpec(memory_space=pl.ANY), pl.BlockSpec(memory_space=pl.ANY)],
        out_specs=per_b((rows, w)),
        scratch_shapes=[pltpu.VMEM((2, w, page), F32), pltpu.VMEM((2, w, page), F32),
                        pltpu.SemaphoreType.DMA((2, 2))],
    )
    o = pl.pallas_call(
        functools.partial(_sb_sample_body, n_pages=n_pages, t_new=t),
        out_shape=jax.ShapeDtypeStruct((b, rows, w), F32),
        grid_spec=grid_spec,
        compiler_params=_params(("arbitrary",)),
        name="sb_sample",
    )(page_table, q_all, kn, vn, _tri(page), head_mask, cache_kt, cache_vt)
    return _collapse_heads(o, t)


def _moba_select_body(pt_ref, q_ref, *refs, n_blocks):
    k_refs = refs[:PAGES_PER_STEP]
    o_ref = refs[PAGES_PER_STEP]
    km_ref = refs[PAGES_PER_STEP + 1]
    step = pl.program_id(1)
    page = k_refs[0].shape[1]
    pages_per_block = MOBA_BLOCK // page
    lane = lax.broadcasted_iota(jnp.int32, km_ref.shape, 1)

    @pl.when(step == 0)
    def _():
        km_ref[...] = jnp.zeros(km_ref.shape, F32)

    km = km_ref[...]
    for blk in range(PAGES_PER_STEP // pages_per_block):
        total = k_refs[blk * pages_per_block][...]
        for i in range(1, pages_per_block):
            total = total + k_refs[blk * pages_per_block + i][...]
        t_hi = total.astype(BF16)
        rest = total - t_hi.astype(F32)
        t_mid = rest.astype(BF16)
        t_lo = (rest - t_mid.astype(F32)).astype(BF16)
        ones = jnp.ones((page, LANES), BF16)
        mean = (_dot(t_hi, ones) + _dot(t_mid, ones) + _dot(t_lo, ones)) * (1.0 / MOBA_BLOCK)
        km = jnp.where(lane == step * (PAGES_PER_STEP // pages_per_block) + blk, mean, km)
    km_ref[...] = km

    @pl.when(step == pl.num_programs(1) - 1)
    def _():
        q = q_ref[0]
        rows = q.shape[0]
        km_hi, km_lo = _split(km)
        gate = _dot(q, km_hi) + _dot(q, km_lo)
        blk_lane = lax.broadcasted_iota(jnp.int32, (rows, LANES), 1)
        lane_f = blk_lane.astype(F32)
        gate = jnp.where(blk_lane < n_blocks, gate, -jnp.inf)
        chosen = jnp.zeros((rows, LANES), F32)
        for _k in range(min(MOBA_TOPK, n_blocks)):
            best = jnp.max(gate, axis=1, keepdims=True)
            first = jnp.min(jnp.where(gate == best, lane_f, float(LANES)), axis=1, keepdims=True)
            pick = lane_f == first
            chosen = jnp.where(pick, 1.0, chosen)
            gate = jnp.where(pick, -jnp.inf, gate)
        o_ref[0] = chosen


def _moba_select(q_all, cache_kt, page_table):
    b, n_pages = page_table.shape
    w, page = cache_kt.shape[1:]
    rows = q_all.shape[1]
    steps = n_pages // PAGES_PER_STEP
    n_blocks = n_pages * page // MOBA_BLOCK
    assert n_blocks <= LANES
    grid_spec = pltpu.PrefetchScalarGridSpec(
        num_scalar_prefetch=1,
        grid=(b, steps),
        in_specs=[pl.BlockSpec((1, rows, w), lambda bi, s, pt: (bi, 0, 0))] + _page_specs(n_pages, page, w),
        out_specs=pl.BlockSpec((1, rows, LANES), lambda bi, s, pt: (bi, 0, 0)),
        scratch_shapes=[pltpu.VMEM((w, LANES), F32)],
    )
    return pl.pallas_call(
        functools.partial(_moba_select_body, n_blocks=n_blocks),
        out_shape=jax.ShapeDtypeStruct((b, rows, LANES), F32),
        grid_spec=grid_spec,
        compiler_params=_params(("arbitrary", "arbitrary")),
        name="moba_select",
    )(page_table, q_all, *([cache_kt] * PAGES_PER_STEP))


def _moba_sample_body(pt_ref, q_ref, kn_ref, vn_ref, sel_ref, hm_ref, *rest, t_new):
    k_refs = rest[:PAGES_PER_STEP]
    v_refs = rest[PAGES_PER_STEP:2 * PAGES_PER_STEP]
    o_ref = rest[2 * PAGES_PER_STEP]
    m_ref, l_ref, acc_ref = rest[2 * PAGES_PER_STEP + 1:]
    step = pl.program_id(1)
    q = q_ref[0]
    rows = q.shape[0]
    page = k_refs[0].shape[1]
    lane = lax.broadcasted_iota(jnp.int32, (rows, LANES), 1)

    @pl.when(step == 0)
    def _():
        kn = kn_ref[0].astype(BF16)
        vn = vn_ref[0].astype(BF16)
        qt = lax.broadcasted_iota(jnp.int32, (rows, LANES), 0) % t_new
        ok = lane <= qt
        s = jnp.where(ok, _dot_nt(q, kn), NEG_BIG)
        m = jnp.max(s, axis=1, keepdims=True)
        p = jnp.where(ok, jnp.exp(s - m), 0.0)
        m_ref[...] = jnp.broadcast_to(m, (rows, LANES))
        l_ref[...] = jnp.broadcast_to(jnp.sum(p, axis=1, keepdims=True), (rows, LANES))
        acc_ref[...] = _dot(p.astype(BF16), vn)

    chosen = sel_ref[0]
    pages_per_block = MOBA_BLOCK // page
    scores = []
    for i in range(PAGES_PER_STEP):
        blk = (step * PAGES_PER_STEP + i) // pages_per_block
        picked = jnp.sum(jnp.where(lane == blk, chosen, 0.0), axis=1, keepdims=True)
        ok = jnp.broadcast_to(picked, (rows, page)) > 0.5
        scores.append(jnp.where(ok, _dot(q, k_refs[i][...].astype(BF16)), NEG_BIG))
    m = m_ref[:, 0:1]
    m_new = m
    for s in scores:
        m_new = jnp.maximum(m_new, jnp.max(s, axis=1, keepdims=True))
    alpha = jnp.exp(m - m_new)
    l = alpha * l_ref[:, 0:1]
    acc = alpha * acc_ref[...]
    for i, s in enumerate(scores):
        p = jnp.where(s > 0.5 * NEG_BIG, jnp.exp(s - m_new), 0.0)
        l = l + jnp.sum(p, axis=1, keepdims=True)
        acc = acc + _dot_nt(p.astype(BF16), v_refs[i][...].astype(BF16))
    m_ref[...] = jnp.broadcast_to(m_new, (rows, LANES))
    l_ref[...] = jnp.broadcast_to(l, (rows, LANES))
    acc_ref[...] = acc

    @pl.when(step == pl.num_programs(1) - 1)
    def _():
        o_ref[0] = (acc / l) * hm_ref[...]


def _moba_sample(q_all, head_mask, k_new, v_new, chosen, cache_kt, cache_vt, page_table, t_new):
    b, rows, w = q_all.shape
    n_pages = page_table.shape[1]
    page = cache_kt.shape[2]
    assert page == LANES
    kn = _pad_rows(k_new, LANES)
    vn = _pad_rows(v_new, LANES)
    specs = _page_specs(n_pages, page, w)
    per_b = lambda shp: pl.BlockSpec((1,) + shp, lambda bi, s, pt: (bi, 0, 0))
    const = lambda shp: pl.BlockSpec(shp, lambda bi, s, pt: (0, 0))
    grid_spec = pltpu.PrefetchScalarGridSpec(
        num_scalar_prefetch=1,
        grid=(b, n_pages // PAGES_PER_STEP),
        in_specs=[per_b((rows, w)), per_b((LANES, w)), per_b((LANES, w)), per_b((rows, LANES)),
                  const((rows, w))] + specs + specs,
        out_specs=per_b((rows, w)),
        scratch_shapes=[pltpu.VMEM((rows, LANES), F32), pltpu.VMEM((rows, LANES), F32),
                        pltpu.VMEM((rows, w), F32)],
    )
    o = pl.pallas_call(
        functools.partial(_moba_sample_body, t_new=t_new),
        out_shape=jax.ShapeDtypeStruct((b, rows, w), F32),
        grid_spec=grid_spec,
        compiler_params=_params(("arbitrary", "arbitrary")),
        name="moba_sample",
    )(page_table, q_all, kn, vn, chosen, head_mask, *([cache_kt] * PAGES_PER_STEP),
      *([cache_vt] * PAGES_PER_STEP))
    return _collapse_heads(o, t_new)


def _merge_body(osb_ref, omb_ref, sgsb_ref, sgmb_ref, x_ref, ga_ref, sh_ref, sc_ref, g2_ref, wsb_ref,
                wmb_ref, wout_ref, wr_ref, br_ref, x1_ref, h2_ref, eid_ref, gate_ref):
    u = (sgsb_ref[0].astype(F32) * _dot(osb_ref[0], wsb_ref[...])
         + sgmb_ref[0].astype(F32) * _dot(omb_ref[0], wmb_ref[...]))
    x1 = x_ref[0] + ga_ref[0] * _dot(u.astype(BF16), wout_ref[...])
    x1_ref[0] = x1
    h2 = _modulate(x1, g2_ref[...], sh_ref[0], sc_ref[0])
    h2_ref[0] = h2
    logits = _dot3(h2, wr_ref[...]) + br_ref[...]
    rows = logits.shape[0]
    lane = lax.broadcasted_iota(jnp.int32, (rows, LANES), 1)
    lane_f = lane.astype(F32)
    eid = jnp.zeros((rows, LANES), F32)
    top = jnp.zeros((rows, LANES), F32)
    best0 = None
    for k in range(TOP_K):
        best = jnp.max(logits, axis=1, keepdims=True)
        first = jnp.min(jnp.where(logits == best, lane_f, float(LANES)), axis=1, keepdims=True)
        logits = jnp.where(lane_f == first, -jnp.inf, logits)
        if best0 is None:
            best0 = best
        eid = jnp.where(lane == k, first, eid)
        top = jnp.where(lane == k, jnp.exp(best - best0), top)
    eid_ref[0] = eid.astype(jnp.int32)
    gate_ref[0] = top / jnp.sum(top, axis=1, keepdims=True)


def _merge(o_sb, o_mb, sg_sb, sg_mb, x, ga1, sh2, sc2, g2, w_sb16, w_mb16, w_out16, w_router_pad,
           b_router_pad, tm):
    b, t, d = x.shape
    assert d == SUBLANES * LANES
    r = ga1.shape[1]
    nt = t // tm
    tile = lambda wd: pl.BlockSpec((1, tm, wd), lambda bi, ti: (bi, ti, 0))
    const = lambda shp: pl.BlockSpec(shp, lambda bi, ti: (0,) * len(shp))
    mod = pl.BlockSpec((1, r, d), lambda bi, ti: (bi, 0, 0))
    return pl.pallas_call(
        _merge_body,
        out_shape=[jax.ShapeDtypeStruct((b, t, d), F32), jax.ShapeDtypeStruct((b, t, d), F32),
                   jax.ShapeDtypeStruct((b, t, LANES), jnp.int32), jax.ShapeDtypeStruct((b, t, LANES), F32)],
        grid=(b, nt),
        in_specs=[tile(WIDTH), tile(WIDTH), tile(d), tile(d), tile(d), mod, mod, mod, const((1, d)),
                  const((WIDTH, d)), const((WIDTH, d)), const((d, d)), const((d, LANES)), const((1, LANES))],
        out_specs=[tile(d), tile(d), tile(LANES), tile(LANES)],
        compiler_params=_params(("arbitrary", "arbitrary")),
        name="merge",
    )(o_sb, o_mb, sg_sb, sg_mb, x, ga1, sh2, sc2, g2, w_sb16, w_mb16, w_out16, w_router_pad, b_router_pad)


def _ffn_body(be_ref, nu_ref, tok_ref, tokn_ref, dst_ref, dstp_ref, h2_hbm, wgu_ref, bgu_ref, wd_ref, bd_ref,
              y_hbm, xbuf, obuf, wgu16, wd16, gsem, ssem, *, tm, ff):
    j = pl.program_id(0)
    n_used = nu_ref[0]
    cur = j % 2
    groups = tm // SUBLANES

    def gather_row(idx_ref, grp, sub, buf):
        tok = idx_ref[0, 0, grp * SUBLANES + sub]
        return pltpu.make_async_copy(h2_hbm.at[tok >> 3, :, tok & 7, :], xbuf.at[buf, grp, :, sub, :],
                                     gsem.at[buf])

    def scatter_row(idx_ref, grp, sub, buf):
        dst = idx_ref[0, 0, grp * SUBLANES + sub]
        return pltpu.make_async_copy(obuf.at[buf, grp, :, sub, :], y_hbm.at[dst >> 3, :, dst & 7, :],
                                     ssem.at[0])

    def for_each_group(fn):
        def body(grp, c):
            for sub in range(SUBLANES):
                fn(grp, sub)
            return c
        lax.fori_loop(0, groups, body, 0)

    def wait_gather(buf):
        pltpu.make_async_copy(h2_hbm.at[pl.ds(0, groups)], xbuf.at[buf], gsem.at[buf]).wait()

    def wait_scatter(buf):
        pltpu.make_async_copy(obuf.at[buf], y_hbm.at[pl.ds(0, groups)], ssem.at[0]).wait()

    @pl.when(j == 0)
    def _():
        n_slot_groups = y_hbm.shape[0] - 2 * groups
        obuf[...] = jnp.zeros(obuf.shape, F32)
        for half in range(2):
            fill = pltpu.make_async_copy(obuf.at[0], y_hbm.at[pl.ds(n_slot_groups + half * groups, groups)],
                                         ssem.at[0])
            fill.start()
            fill.wait()
        for_each_group(lambda grp, sub: gather_row(tok_ref, grp, sub, 0).start())

    @pl.when(j < n_used)
    def _():
        @pl.when(j >= 1)
        def _():
            wait_scatter(cur)

        @pl.when((j == 0) | (be_ref[j] != be_ref[jnp.maximum(j - 1, 0)]))
        def _():
            wgu16[...] = wgu_ref[0].astype(BF16)
            wd16[...] = wd_ref[0].astype(BF16)

        wait_gather(cur)
        x16 = jnp.concatenate([xbuf[cur, :, c, :, :].reshape(tm, LANES).astype(BF16)
                               for c in range(SUBLANES)], axis=1)
        y = jnp.zeros((tm, wd16.shape[1]), F32) + bd_ref[0]
        chunk = 512
        n_chunks = ff // chunk
        rows_per_chunk = tm // n_chunks
        for c in range(n_chunks):
            g = _dot(x16, wgu16[:, c * chunk:(c + 1) * chunk]) + bgu_ref[0, :, c * chunk:(c + 1) * chunk]
            u = (_dot(x16, wgu16[:, ff + c * chunk:ff + (c + 1) * chunk])
                 + bgu_ref[0, :, ff + c * chunk:ff + (c + 1) * chunk])
            g = jnp.minimum(g, SWIGLU_LIMIT)
            u = jnp.clip(u, -SWIGLU_LIMIT, SWIGLU_LIMIT)
            act = (u + 1.0) * (g * jax.nn.sigmoid(SWIGLU_ALPHA * g))
            y = y + _dot(act.astype(BF16), wd16[c * chunk:(c + 1) * chunk, :])
            for r in range(c * rows_per_chunk, (c + 1) * rows_per_chunk):
                gather_row(tokn_ref, r // SUBLANES, r % SUBLANES, 1 - cur).start()
                scatter_row(dstp_ref, r // SUBLANES, r % SUBLANES, 1 - cur).start()

        for c in range(SUBLANES):
            obuf[cur, :, c, :, :] = y[:, c * LANES:(c + 1) * LANES].reshape(groups, SUBLANES, LANES)

        @pl.when(j == n_used - 1)
        def _():
            wait_scatter(1 - cur)
            wait_gather(1 - cur)
            for_each_group(lambda grp, sub: scatter_row(dst_ref, grp, sub, cur).start())
            wait_scatter(cur)


def _row_groups(x):
    rows, d = x.shape
    return x.reshape(rows // SUBLANES, SUBLANES, d // LANES, LANES).transpose(0, 2, 1, 3)


def _expert_ffn(h2_rows, row_tok, row_dst, row_dst_prev, block_e, n_used, w_gu, b_gu, w_down, b_down,
                n_out_rows):
    n_blocks, _, tm = row_tok.shape
    e, d, ff2 = w_gu.shape
    ff = ff2 // 2
    assert d == SUBLANES * LANES
    groups = tm // SUBLANES

    def blk(j, be, nu):
        return (jnp.minimum(j, nu[0] - 1), 0, 0)

    def blk_next(j, be, nu):
        return (jnp.minimum(j + 1, nu[0] - 1), 0, 0)

    smem = lambda index: pl.BlockSpec((1, 1, tm), index, memory_space=pltpu.SMEM)
    per_e = lambda shp: pl.BlockSpec((1,) + shp, lambda j, be, nu: (be[j], 0, 0))
    grid_spec = pltpu.PrefetchScalarGridSpec(
        num_scalar_prefetch=2,
        grid=(n_blocks,),
        in_specs=[smem(blk), smem(blk_next), smem(blk), smem(blk), pl.BlockSpec(memory_space=pl.ANY),
                  per_e((d, ff2)), per_e((1, ff2)), per_e((ff, d)), per_e((1, d))],
        out_specs=pl.BlockSpec(memory_space=pl.ANY),
        scratch_shapes=[pltpu.VMEM((2, groups, SUBLANES, SUBLANES, LANES), F32),
                        pltpu.VMEM((2, groups, SUBLANES, SUBLANES, LANES), F32),
                        pltpu.VMEM((d, ff2), BF16), pltpu.VMEM((ff, d), BF16),
                        pltpu.SemaphoreType.DMA((2,)), pltpu.SemaphoreType.DMA((1,))],
    )
    y = pl.pallas_call(
        functools.partial(_ffn_body, tm=tm, ff=ff),
        out_shape=jax.ShapeDtypeStruct((n_out_rows // SUBLANES, SUBLANES, SUBLANES, LANES), F32),
        grid_spec=grid_spec,
        compiler_params=_params(("arbitrary",), vmem=56 * 1024 * 1024),
        name="expert_ffn",
    )(block_e, n_used, row_tok, row_tok, row_dst, row_dst_prev, _row_groups(h2_rows), w_gu,
      b_gu.reshape(e, 1, ff2), w_down, b_down.reshape(e, 1, d))
    return y.transpose(0, 2, 1, 3).reshape(n_out_rows, d)


def _route(eid, n_tokens):
    tile = EXPERT_TILE
    nk = n_tokens * TOP_K
    n_blocks = (nk + N_EXPERTS * (tile - 1) + tile - 1) // tile
    rows = n_blocks * tile
    onehot = (eid[:, :, None] == jnp.arange(N_EXPERTS, dtype=jnp.int32)[None, None, :]).astype(jnp.int32)
    per_token = onehot.sum(axis=1)
    before = jnp.cumsum(per_token, axis=0) - per_token
    counts = per_token.sum(axis=0)
    padded = (counts + tile - 1) // tile * tile
    pend = jnp.cumsum(padded)
    pstart = pend - padded
    dest = ((before + pstart[None, :])[:, None, :] * onehot).sum(axis=2).reshape(-1).astype(jnp.int32)
    pos = jnp.arange(rows, dtype=jnp.int32)
    row_flat = jnp.full((rows,), -1, jnp.int32).at[dest].set(
        jnp.arange(nk, dtype=jnp.int32), unique_indices=True)
    is_pad = row_flat < 0
    dump = nk + ((pos // tile) % 2) * tile + pos % tile
    row_tok = jnp.where(is_pad, 0, row_flat // TOP_K).astype(jnp.int32)
    row_dst = jnp.where(is_pad, dump, (row_flat % TOP_K) * n_tokens + row_flat // TOP_K).astype(jnp.int32)
    n_used = (pend[-1] // tile).astype(jnp.int32)
    blk_start = jnp.minimum(jnp.arange(n_blocks), n_used - 1) * tile
    block_e = jnp.minimum((pend[None, :] <= blk_start[:, None]).sum(axis=1), N_EXPERTS - 1).astype(jnp.int32)
    row_dst = row_dst.reshape(n_blocks, 1, tile)
    first = (nk + tile + jnp.arange(tile, dtype=jnp.int32)).reshape(1, 1, tile)
    row_dst_prev = jnp.concatenate([first, row_dst[:-1]], axis=0)
    return (row_tok.reshape(n_blocks, 1, tile), row_dst, row_dst_prev, block_e, n_used.reshape(1),
            nk + 2 * tile)


def _combine_body(y0_ref, y1_ref, y2_ref, y3_ref, gate_ref, x1_ref, ga_ref, o_ref):
    gate = gate_ref[0]
    ffn = None
    for k, y_ref in enumerate((y0_ref, y1_ref, y2_ref, y3_ref)):
        part = gate[:, k:k + 1] * y_ref[...]
        ffn = part if ffn is None else ffn + part
    o_ref[0] = x1_ref[0] + ga_ref[0] * ffn


def _combine(y, gate, x1, ga2, tm, first_token, n_tokens):
    b, t, d = x1.shape
    r = ga2.shape[1]
    nt = t // tm
    assert first_token % tm == 0 and n_tokens % tm == 0
    tile = lambda wd: pl.BlockSpec((1, tm, wd), lambda bi, ti: (bi, ti, 0))

    def y_spec(k):
        base = (k * n_tokens + first_token) // tm
        return pl.BlockSpec((tm, d), lambda bi, ti: (base + bi * nt + ti, 0))

    return pl.pallas_call(
        _combine_body,
        out_shape=jax.ShapeDtypeStruct((b, t, d), F32),
        grid=(b, nt),
        in_specs=[y_spec(k) for k in range(TOP_K)]
        + [tile(LANES), tile(d), pl.BlockSpec((1, r if r == 1 else tm, d),
                                              lambda bi, ti: (bi, 0, 0) if r == 1 else (bi, ti, 0))],
        out_specs=tile(d),
        compiler_params=_params(("arbitrary", "arbitrary")),
        name="combine",
    )(y, y, y, y, gate, x1, ga2)


def _head_gain(g):
    return jnp.tile(g.astype(F32), N_HEADS).reshape(1, WIDTH)


def _mods(mod, d):
    return [mod[:, None, i * d:(i + 1) * d] for i in range(N_MOD)]


def kernel(x_prompt, x_sample, c_prompt, c_sample, cache_sb_k, cache_sb_v, cache_mb_k, cache_mb_v, page_table,
           w_ada, b_ada, g_norm1, g_norm2, w_in, q_norm_g, k_norm_g, w_br_sb, w_br_mb, w_out, w_router,
           b_router, w_gu, b_gu, w_down, b_down):
    depth = w_ada.shape[0]
    assert depth == 1
    bp, seq, d = x_prompt.shape
    bs, t_new, _ = x_sample.shape
    page = cache_sb_k.shape[2]
    past = page_table.shape[1] * page
    n_prompt = bp * seq
    n_sample = bs * t_new
    lyr = 0

    mod = _adaln(jnp.concatenate([c_prompt, c_sample], axis=0), w_ada[lyr], b_ada[lyr])
    sh1p, sc1p, ga1p, sh2p, sc2p, ga2p = _mods(mod[:bp], d)
    per_row = lambda m: jnp.repeat(m, t_new, axis=0).reshape(1, n_sample, d)
    sh1s, sc1s, ga1s, sh2s, sc2s, ga2s = [per_row(m[:, 0]) for m in _mods(mod[bp:], d)]

    w_in16 = w_in[lyr].astype(BF16)
    g1 = g_norm1[lyr].reshape(1, d)
    g2 = g_norm2[lyr].reshape(1, d)
    qg = _head_gain(q_norm_g[lyr])
    kg = _head_gain(k_norm_g[lyr])
    head_of = np.arange(WIDTH) // HEAD_DIM
    gmat = jnp.asarray((head_of[:, None] == head_of[None, :]).astype(np.float32) / HEAD_DIM, dtype=BF16)

    pos_p = jnp.arange(seq, dtype=jnp.int32)
    (qsb, ksb, vsb, ksb16, vsb16, qmb, kmb, vmb, kmb16, _, sgsb, sgmb, kmean, vmbt16) = _inproj(
        x_prompt, sh1p, sc1p, g1, w_in16, pos_p, qg, kg, gmat, ROW_TILE, True)
    o_sb_p = _sb_prompt(qsb, ksb16, vsb16)
    o_mb_p = _moba_prompt(qmb, kmb16, vmbt16, kmean)

    xs = x_sample.reshape(1, n_sample, d)
    pos_s = past + jnp.arange(n_sample, dtype=jnp.int32) % t_new
    (qsb_s, ksb_s, vsb_s, _, _, qmb_s, kmb_s, vmb_s, _, _, sgsb_s, sgmb_s) = _inproj(
        xs, sh1s, sc1s, g1, w_in16, pos_s, qg, kg, gmat, n_sample, False)
    by_batch = lambda a: a.reshape(bs, t_new, WIDTH)
    o_sb_s = _sb_sample(by_batch(qsb_s), by_batch(ksb_s), by_batch(vsb_s), _pages_t(cache_sb_k, lyr),
                        _pages_t(cache_sb_v, lyr), page_table)
    mb_kt = _pages_t(cache_mb_k, lyr)
    q_all_mb, head_mask = _expand_heads(by_batch(qmb_s))
    chosen = _moba_select(q_all_mb, mb_kt, page_table)
    o_mb_s = _moba_sample(q_all_mb, head_mask, by_batch(kmb_s), by_batch(vmb_s), chosen, mb_kt,
                          _pages_t(cache_mb_v, lyr), page_table, t_new)
    o_sb_s = o_sb_s.reshape(1, n_sample, WIDTH).astype(BF16)
    o_mb_s = o_mb_s.reshape(1, n_sample, WIDTH).astype(BF16)

    w_sb16 = w_br_sb[lyr].astype(BF16)
    w_mb16 = w_br_mb[lyr].astype(BF16)
    w_out16 = w_out[lyr].astype(BF16)
    w_router_pad = jnp.pad(w_router[lyr], ((0, 0), (0, LANES - N_EXPERTS)))
    b_router_pad = jnp.concatenate([b_router[lyr].astype(F32),
                                    jnp.full((LANES - N_EXPERTS,), -jnp.inf, F32)]).reshape(1, LANES)
    x1p, h2p, eidp, gatep = _merge(o_sb_p, o_mb_p, sgsb, sgmb, x_prompt, ga1p, sh2p, sc2p, g2, w_sb16,
                                   w_mb16, w_out16, w_router_pad, b_router_pad, ROW_TILE)
    x1s, h2s, eids, gates = _merge(o_sb_s, o_mb_s, sgsb_s, sgmb_s, xs, ga1s, sh2s, sc2s, g2, w_sb16,
                                   w_mb16, w_out16, w_router_pad, b_router_pad, n_sample)

    n_tokens = n_prompt + n_sample
    h2_rows = jnp.concatenate([h2p.reshape(n_prompt, d), h2s.reshape(n_sample, d)])
    eid = jnp.concatenate([eidp.reshape(n_prompt, LANES), eids.reshape(n_sample, LANES)])[:, :TOP_K]
    row_tok, row_dst, row_dst_prev, block_e, n_used, n_out_rows = _route(eid, n_tokens)
    y_rows = _expert_ffn(h2_rows, row_tok, row_dst, row_dst_prev, block_e, n_used, w_gu[lyr], b_gu[lyr],
                         w_down[lyr], b_down[lyr], n_out_rows)
    y_prompt = _combine(y_rows, gatep, x1p, ga2p, COMBINE_TILE, 0, n_tokens)
    y_sample = _combine(y_rows, gates, x1s, ga2s, COMBINE_TILE, n_prompt, n_tokens)

    heads = lambda a, b_, t_: a.reshape(1, b_, t_, N_HEADS, HEAD_DIM)
    return (y_prompt, y_sample.reshape(bs, t_new, d),
            heads(ksb, bp, seq), heads(vsb, bp, seq), heads(kmb, bp, seq), heads(vmb, bp, seq),
            heads(ksb_s, bs, t_new), heads(vsb_s, bs, t_new), heads(kmb_s, bs, t_new), heads(vmb_s, bs, t_new))
```

```python
import functools

import numpy as np
import jax
import jax.numpy as jnp
from jax import lax
from jax.experimental import pallas as pl
from jax.experimental.pallas import tpu as pltpu

HEAD_DIM = 64
N_HEADS = 8
WIDTH = N_HEADS * HEAD_DIM
MOBA_BLOCK = 256
MOBA_TOPK = 3
ROPE_THETA = 10000.0
N_EXPERTS = 32
TOP_K = 4
SWIGLU_LIMIT = 7.0
SWIGLU_ALPHA = 1.702
NORM_EPS = 1e-6
N_MOD = 6
ATTN_SCALE = HEAD_DIM ** -0.5

LANES = 128
SUBLANES = 8
ROW_TILE = 256
ATTN_TILE = 256
EXPERT_TILE = 512
COMBINE_TILE = 128
PAGES_PER_STEP = 8
NEG_BIG = -1e30
STICK_FLOOR = -104.0
VMEM_LIMIT = 48 * 1024 * 1024

F32 = jnp.float32
BF16 = jnp.bfloat16


def _dot(a, b):
    return jnp.dot(a, b, preferred_element_type=F32)


def _dot_nt(a, b):
    return lax.dot_general(a, b, (((1,), (1,)), ((), ())), preferred_element_type=F32)


def _split(x):
    hi = x.astype(BF16)
    lo = (x - hi.astype(F32)).astype(BF16)
    return hi, lo


def _dot3(a, b):
    a_hi, a_lo = _split(a)
    b_hi, b_lo = _split(b)
    return _dot(a_hi, b_hi) + _dot(a_lo, b_hi) + _dot(a_hi, b_lo)


def _params(sem, vmem=VMEM_LIMIT):
    return pltpu.CompilerParams(dimension_semantics=sem, vmem_limit_bytes=vmem)


def _adaln_body(c_ref, w_ref, b_ref, o_ref):
    c = c_ref[...]
    s = c * jax.nn.sigmoid(c)
    o_ref[...] = _dot3(s, w_ref[...]) + b_ref[...]


def _adaln(c, w_ada, b_ada):
    rows, d = c.shape
    cols = w_ada.shape[1]
    tn = 1024
    return pl.pallas_call(
        _adaln_body,
        out_shape=jax.ShapeDtypeStruct((rows, cols), F32),
        grid=(cols // tn,),
        in_specs=[pl.BlockSpec((rows, d), lambda j: (0, 0)),
                  pl.BlockSpec((d, tn), lambda j: (0, j)),
                  pl.BlockSpec((1, tn), lambda j: (0, j))],
        out_specs=pl.BlockSpec((rows, tn), lambda j: (0, j)),
        compiler_params=_params(("arbitrary",)),
        name="adaln",
    )(c, w_ada, b_ada.reshape(1, cols))


def _modulate(x, g, shift, scale):
    ms = jnp.mean(x * x, axis=-1, keepdims=True)
    return (x * lax.rsqrt(ms + NORM_EPS) * g) * (1.0 + scale) + shift


def _qk_norm_rope(p, g, gmat, cos, sin_signed):
    ms = _dot((p * p).astype(BF16), gmat)
    y = p * lax.rsqrt(ms + NORM_EPS) * g
    lane = lax.broadcasted_iota(jnp.int32, (p.shape[0], LANES), 1)
    first_half = (lane & (HEAD_DIM // 2)) == 0
    outs = []
    for c in range(WIDTH // LANES):
        yc = y[:, c * LANES:(c + 1) * LANES]
        partner = jnp.where(first_half, pltpu.roll(yc, LANES - HEAD_DIM // 2, 1),
                            pltpu.roll(yc, HEAD_DIM // 2, 1))
        outs.append(yc * cos + partner * sin_signed)
    return jnp.concatenate(outs, axis=1)


def _inproj_body(emit_kmean, x_ref, sh_ref, sc_ref, g1_ref, w_ref, cos_ref, sin_ref, qg_ref, kg_ref,
                 gmat_ref, qsb_ref, ksb_ref, vsb_ref, ksb16_ref, vsb16_ref, qmb_ref, kmb_ref, vmb_ref,
                 kmb16_ref, vmb16_ref, sgsb_ref, sgmb_ref, *rest):
    h = _modulate(x_ref[0], g1_ref[...], sh_ref[0], sc_ref[0])
    h16 = h.astype(BF16)

    def proj(c0, width):
        return _dot(h16, w_ref[:, c0:c0 + width])

    w = WIDTH
    qsb_ref[0] = (proj(0, w) * ATTN_SCALE).astype(BF16)
    ksb = proj(w, w)
    ksb_ref[0] = ksb
    ksb16_ref[0] = ksb.astype(BF16)
    vsb = proj(2 * w, w)
    vsb_ref[0] = vsb
    vsb16_ref[0] = vsb.astype(BF16)
    cos = cos_ref[...]
    sin = sin_ref[...]
    gmat = gmat_ref[...]
    qmb = _qk_norm_rope(proj(3 * w, w), qg_ref[...], gmat, cos, sin)
    qmb_ref[0] = (qmb * ATTN_SCALE).astype(BF16)
    kmb = _qk_norm_rope(proj(4 * w, w), kg_ref[...], gmat, cos, sin)
    kmb_ref[0] = kmb
    kmb16_ref[0] = kmb.astype(BF16)
    vmb = proj(5 * w, w)
    vmb_ref[0] = vmb
    vmb16_ref[0] = vmb.astype(BF16)
    d = x_ref.shape[2]
    sgsb_ref[0] = jax.nn.sigmoid(proj(6 * w, d)).astype(BF16)
    sgmb_ref[0] = jax.nn.sigmoid(proj(6 * w + d, d)).astype(BF16)
    if emit_kmean:
        kmean_ref, vmbt_ref = rest
        ti = pl.program_id(1)
        kmean_ref[0, pl.ds(ti, 1), :] = jnp.mean(kmb, axis=0, keepdims=True)
        vmbt_ref[0, 0] = vmb.T.astype(BF16)


def _rope_tables(pos):
    inv_freq = ROPE_THETA ** (-jnp.arange(0, HEAD_DIM, 2, dtype=F32) / HEAD_DIM)
    ang = pos.astype(F32)[:, None] * inv_freq[None, :]
    cos = jnp.tile(jnp.cos(ang), (1, LANES // (HEAD_DIM // 2)))
    sin = jnp.sin(ang)
    sin_signed = jnp.tile(jnp.concatenate([-sin, sin], axis=1), (1, LANES // HEAD_DIM))
    return cos, sin_signed


def _inproj(x, shift, scale, g1, w_in16, pos, qg, kg, gmat, tm, emit_kmean):
    b, t, d = x.shape
    r = shift.shape[1]
    cols = w_in16.shape[1]
    nt = t // tm
    cos, sin_signed = _rope_tables(pos)
    row = lambda dt, wd: jax.ShapeDtypeStruct((b, t, wd), dt)
    tile = lambda wd: pl.BlockSpec((1, tm, wd), lambda bi, ti: (bi, ti, 0))
    const = lambda shp: pl.BlockSpec(shp, lambda bi, ti: (0,) * len(shp))
    mod = pl.BlockSpec((1, r, d), lambda bi, ti: (bi, 0, 0))
    out_shape = [row(BF16, WIDTH), row(F32, WIDTH), row(F32, WIDTH), row(BF16, WIDTH), row(BF16, WIDTH),
                 row(BF16, WIDTH), row(F32, WIDTH), row(F32, WIDTH), row(BF16, WIDTH), row(BF16, WIDTH),
                 row(BF16, d), row(BF16, d)]
    out_specs = [tile(WIDTH)] * 10 + [tile(d)] * 2
    if emit_kmean:
        assert tm == MOBA_BLOCK
        out_shape.append(jax.ShapeDtypeStruct((b, nt, WIDTH), F32))
        out_specs.append(pl.BlockSpec((1, nt, WIDTH), lambda bi, ti: (bi, 0, 0)))
        out_shape.append(jax.ShapeDtypeStruct((b, nt, WIDTH, tm), BF16))
        out_specs.append(pl.BlockSpec((1, 1, WIDTH, tm), lambda bi, ti: (bi, ti, 0, 0)))
    return pl.pallas_call(
        functools.partial(_inproj_body, emit_kmean),
        out_shape=out_shape,
        grid=(b, nt),
        in_specs=[tile(d), mod, mod, const((1, d)), const((d, cols)),
                  pl.BlockSpec((tm, LANES), lambda bi, ti: (ti, 0)),
                  pl.BlockSpec((tm, LANES), lambda bi, ti: (ti, 0)),
                  const((1, WIDTH)), const((1, WIDTH)), const((WIDTH, WIDTH))],
        out_specs=out_specs,
        compiler_params=_params(("arbitrary", "arbitrary")),
        name="inproj_kmean" if emit_kmean else "inproj",
    )(x, shift, scale, g1, w_in16, cos, sin_signed, qg, kg, gmat)


def _stick_tile(z, allowed, tri, carry):
    sp = jnp.maximum(z, 0.0) + jnp.log(1.0 + jnp.exp(-jnp.abs(z)))
    log_keep = -sp
    if allowed is not None:
        log_keep = jnp.where(allowed, log_keep, 0.0)
    lk_hi, lk_lo = _split(log_keep)
    later = _dot(lk_hi, tri) + _dot(lk_lo, tri)
    ones = jnp.ones((z.shape[1], LANES), BF16)
    total = _dot(lk_hi, ones) + _dot(lk_lo, ones)
    reps = z.shape[1] // LANES
    carry_wide = carry if reps == 1 else jnp.concatenate([carry] * reps, axis=1)
    a = jnp.exp((z - sp) + later + carry_wide)
    if allowed is not None:
        a = jnp.where(allowed, a, 0.0)
    return a, carry + total


def _head_masks(rows):
    lane = lax.broadcasted_iota(jnp.int32, (rows, LANES), 1)
    return lane < HEAD_DIM


def _sb_prompt_body(q_ref, k_ref, v_ref, tri_ref, o_ref, *, tile):
    qi = pl.program_id(2)
    q = q_ref[0]
    tri = tri_ref[...]
    low = _head_masks(tile)
    row = lax.broadcasted_iota(jnp.int32, (tile, tile), 0)
    col = lax.broadcasted_iota(jnp.int32, (tile, tile), 1)
    diag_allowed = col < row
    zeros16 = jnp.zeros_like(q)
    qms = (jnp.where(low, q, zeros16), jnp.where(low, zeros16, q))

    def logits(kb):
        start = pl.multiple_of(kb * tile, tile)
        k_blk = k_ref[0, pl.ds(start, tile), :]
        return tuple(_dot_nt(qms[h], k_blk) for h in range(2))

    def key_tile(kb, z, state, allowed):
        start = pl.multiple_of(kb * tile, tile)
        v_blk = v_ref[0, pl.ds(start, tile), :]
        out = []
        for h in range(2):
            carry, acc = state[h]
            a, carry = _stick_tile(z[h], allowed, tri, carry)
            out.append((carry, acc + _dot(a.astype(BF16), v_blk)))
        return tuple(out)

    zero = jnp.zeros((tile, LANES), F32)
    z_next = logits(jnp.maximum(qi - 1, 0))
    state = key_tile(qi, logits(qi), ((zero, zero), (zero, zero)), diag_allowed)

    def live(st):
        return jnp.maximum(jnp.max(st[0][0]), jnp.max(st[1][0])) > STICK_FLOOR

    def cond(c):
        return (c[0] < qi) & live(c[2])

    def body(c):
        i, z, st = c
        kb = qi - 1 - i
        z_after = logits(jnp.maximum(kb - 1, 0))
        return i + 1, z_after, key_tile(kb, z, st, None)

    _, _, state = lax.while_loop(cond, body, (jnp.int32(0), z_next, state))
    o_ref[0] = jnp.where(low, state[0][1], state[1][1]).astype(o_ref.dtype)


def _tri(n):
    j = np.arange(n)[:, None]
    s = np.arange(n)[None, :]
    return jnp.asarray((j > s).astype(np.float32), dtype=BF16)


def _sb_prompt(q16, k16, v16):
    b, s, w = q16.shape
    tile = ATTN_TILE
    pairs = w // LANES
    return pl.pallas_call(
        functools.partial(_sb_prompt_body, tile=tile),
        out_shape=jax.ShapeDtypeStruct((b, s, w), BF16),
        grid=(b, pairs, s // tile),
        in_specs=[pl.BlockSpec((1, tile, LANES), lambda bi, p, qi: (bi, qi, p)),
                  pl.BlockSpec((1, s, LANES), lambda bi, p, qi: (bi, 0, p)),
                  pl.BlockSpec((1, s, LANES), lambda bi, p, qi: (bi, 0, p)),
                  pl.BlockSpec((tile, tile), lambda bi, p, qi: (0, 0))],
        out_specs=pl.BlockSpec((1, tile, LANES), lambda bi, p, qi: (bi, qi, p)),
        compiler_params=_params(("arbitrary",) * 3),
        name="sb_prompt",
    )(q16, k16, v16, _tri(tile))


def _moba_prompt_body(q_ref, k_ref, vt_ref, km_ref, o_ref, sel_ref, *, tile, nb):
    own = pl.program_id(2)
    q = q_ref[0]
    assert nb <= SUBLANES
    km_pad = jnp.concatenate([km_ref[0], jnp.zeros((2 * SUBLANES - nb, LANES), F32)], axis=0)
    km_hi, km_lo = _split(km_pad)
    low = _head_masks(tile)
    zeros16 = jnp.zeros_like(q)
    qms = (jnp.where(low, q, zeros16), jnp.where(low, zeros16, q))
    blk = lax.broadcasted_iota(jnp.int32, (SUBLANES, tile), 0)
    past = blk < own
    for h in range(2):
        gate = (_dot_nt(km_hi, qms[h]) + _dot_nt(km_lo, qms[h]))[:SUBLANES]
        gate = jnp.where(past, gate, -jnp.inf)
        ahead = jnp.zeros((SUBLANES, tile), F32)
        for r in range(1, SUBLANES):
            other = pltpu.roll(gate, r, 0)
            other_blk = (blk - r) & (SUBLANES - 1)
            ahead = ahead + jnp.where(other > gate, 1.0,
                                      jnp.where((other == gate) & (other_blk < blk), 1.0, 0.0))
        sel_ref[h] = jnp.where(past, jnp.where(ahead < MOBA_TOPK, 1.0, 0.0), 0.0)

    key_i = lax.broadcasted_iota(jnp.int32, (tile, tile), 0)
    qry_i = lax.broadcasted_iota(jnp.int32, (tile, tile), 1)
    causal = key_i <= qry_i

    def logits(kb):
        start = pl.multiple_of(kb * tile, tile)
        k_blk = k_ref[0, pl.ds(start, tile), :]
        return tuple(_dot_nt(k_blk, qms[h]) for h in range(2))

    def key_tile(kb, z, state, allowed):
        vt_blk = vt_ref[0, kb]
        out = []
        for h in range(2):
            m, l, acc = state[h]
            ok = allowed
            if ok is None:
                ok = jnp.broadcast_to(sel_ref[h, pl.ds(kb, 1), :], (tile, tile)) > 0.5
            s = jnp.where(ok, z[h], NEG_BIG)
            m_new = jnp.maximum(m, jnp.max(s, axis=0, keepdims=True))
            p = jnp.where(ok, jnp.exp(s - m_new), 0.0)
            alpha = jnp.exp(m - m_new)
            l = alpha * l + jnp.sum(p, axis=0, keepdims=True)
            acc = alpha * acc + _dot(vt_blk, p.astype(BF16))
            out.append((m_new, l, acc))
        return tuple(out)

    m0 = jnp.full((1, tile), NEG_BIG, F32)
    l0 = jnp.zeros((1, tile), F32)
    a0 = jnp.zeros((LANES, tile), F32)
    z_first = logits(0)
    state = key_tile(own, logits(own), ((m0, l0, a0), (m0, l0, a0)), causal)

    def body(kb, c):
        z_after = logits(jnp.minimum(kb + 1, nb - 1))
        return z_after, key_tile(kb, c[0], c[1], None)

    _, state = lax.fori_loop(0, own, body, (z_first, state))
    outs = [state[h][2] / state[h][1] for h in range(2)]
    dim = lax.broadcasted_iota(jnp.int32, (LANES, tile), 0)
    o_ref[0] = jnp.where(dim < HEAD_DIM, outs[0], outs[1]).T.astype(o_ref.dtype)


def _moba_prompt(q16, k16, vt16, kmean):
    b, s, w = q16.shape
    tile = MOBA_BLOCK
    nb = s // tile
    pairs = w // LANES
    return pl.pallas_call(
        functools.partial(_moba_prompt_body, tile=tile, nb=nb),
        out_shape=jax.ShapeDtypeStruct((b, s, w), BF16),
        grid=(b, pairs, nb),
        in_specs=[pl.BlockSpec((1, tile, LANES), lambda bi, p, qi: (bi, qi, p)),
                  pl.BlockSpec((1, s, LANES), lambda bi, p, qi: (bi, 0, p)),
                  pl.BlockSpec((1, nb, LANES, tile), lambda bi, p, qi: (bi, 0, p, 0)),
                  pl.BlockSpec((1, nb, LANES), lambda bi, p, qi: (bi, 0, p))],
        out_specs=pl.BlockSpec((1, tile, LANES), lambda bi, p, qi: (bi, qi, p)),
        scratch_shapes=[pltpu.VMEM((2, SUBLANES, tile), F32)],
        compiler_params=_params(("arbitrary",) * 3),
        name="moba_prompt",
    )(q16, k16, vt16, kmean)


def _pages_t(cache, layer):
    n_phys, page = cache.shape[1:3]
    return jnp.transpose(cache[layer], (0, 2, 3, 1)).reshape(n_phys, WIDTH, page)


def _page_specs(n_pages, page, width):
    specs = []
    for i in range(PAGES_PER_STEP):
        def index(bi, s, pt, i=i):
            return (pt[bi, s * PAGES_PER_STEP + i], 0, 0)
        specs.append(pl.BlockSpec((None, width, page), index))
    return specs


def _expand_heads(q):
    b, t, w = q.shape
    head_of_lane = jnp.arange(w) // HEAD_DIM
    mask = (head_of_lane[None, :] == jnp.arange(N_HEADS)[:, None]).astype(q.dtype)
    return (q[:, None, :, :] * mask[None, :, None, :]).reshape(b, N_HEADS * t, w), \
        jnp.repeat(mask, t, axis=0).astype(F32)


def _collapse_heads(o, t):
    b, _, w = o.shape
    return o.reshape(b, N_HEADS, t, w).sum(axis=1)


def _pad_rows(x, rows):
    return jnp.pad(x, ((0, 0), (0, rows - x.shape[1]), (0, 0)))


def _sb_sample_body(pt_ref, q_ref, kn_ref, vn_ref, tri_ref, hm_ref, kc_hbm, vc_hbm, o_ref, kbuf, vbuf, sem,
                    *, n_pages, t_new):
    b = pl.program_id(0)
    q = q_ref[0]
    rows = q.shape[0]
    tri = tri_ref[...]

    def page_copies(bi, p, slot):
        phys = pt_ref[bi, p]
        return (pltpu.make_async_copy(kc_hbm.at[phys], kbuf.at[slot], sem.at[0, slot]),
                pltpu.make_async_copy(vc_hbm.at[phys], vbuf.at[slot], sem.at[1, slot]))

    def start(bi, p, slot):
        for c in page_copies(bi, p, slot):
            c.start()

    def wait(bi, p, slot):
        for c in page_copies(bi, p, slot):
            c.wait()

    def slot_of(p):
        return (n_pages - 1 - p) % 2

    @pl.when(b == 0)
    def _():
        start(b, n_pages - 1, 0)

    kn = kn_ref[0].astype(BF16)
    vn = vn_ref[0].astype(BF16)
    key = lax.broadcasted_iota(jnp.int32, (rows, LANES), 1)
    qt = lax.broadcasted_iota(jnp.int32, (rows, LANES), 0) % t_new
    a, carry = _stick_tile(_dot_nt(q, kn), key < qt, tri, jnp.zeros((rows, LANES), F32))
    acc = _dot(a.astype(BF16), vn)

    def cond(c):
        return (c[0] >= 0) & (jnp.max(c[1]) > STICK_FLOOR)

    def body(c):
        p, carry, acc = c
        slot = slot_of(p)
        wait(b, p, slot)

        @pl.when(p >= 1)
        def _():
            start(b, p - 1, 1 - slot)

        k_t = kbuf[slot].astype(BF16)
        v_t = vbuf[slot].astype(BF16)
        a, carry = _stick_tile(_dot(q, k_t), None, tri, carry)
        return p - 1, carry, acc + _dot_nt(a.astype(BF16), v_t)

    p, carry, acc = lax.while_loop(cond, body, (jnp.int32(n_pages - 1), carry, acc))

    @pl.when(p >= 0)
    def _():
        wait(b, p, slot_of(p))

    @pl.when(b + 1 < pl.num_programs(0))
    def _():
        start(b + 1, n_pages - 1, 0)

    o_ref[0] = acc * hm_ref[...]


def _sb_sample(q16, k_new, v_new, cache_kt, cache_vt, page_table):
    b, t, w = q16.shape
    n_pages = page_table.shape[1]
    page = cache_kt.shape[2]
    q_all, head_mask = _expand_heads(q16)
    rows = q_all.shape[1]
    assert page == LANES
    kn = _pad_rows(k_new, LANES)
    vn = _pad_rows(v_new, LANES)
    per_b = lambda shp: pl.BlockSpec((1,) + shp, lambda bi, pt: (bi, 0, 0))
    const = lambda shp: pl.BlockSpec(shp, lambda bi, pt: (0, 0))
    grid_spec = pltpu.PrefetchScalarGridSpec(
        num_scalar_prefetch=1,
        grid=(b,),
        in_specs=[per_b((rows, w)), per_b((LANES, w)), per_b((LANES, w)), const((page, page)),
                  const((rows, w)), pl.BlockSpec(memory_space=pl.ANY), pl.BlockSpec(memory_space=pl.ANY)],
        out_specs=per_b((rows, w)),
        scratch_shapes=[pltpu.VMEM((2, w, page), F32), pltpu.VMEM((2, w, page), F32),
                        pltpu.SemaphoreType.DMA((2, 2))],
    )
    o = pl.pallas_call(
        functools.partial(_sb_sample_body, n_pages=n_pages, t_new=t),
        out_shape=jax.ShapeDtypeStruct((b, rows, w), F32),
        grid_spec=grid_spec,
        compiler_params=_params(("arbitrary",)),
        name="sb_sample",
    )(page_table, q_all, kn, vn, _tri(page), head_mask, cache_kt, cache_vt)
    return _collapse_heads(o, t)


def _moba_select_body(pt_ref, q_ref, *refs, n_blocks):
    k_refs = refs[:PAGES_PER_STEP]
    o_ref = refs[PAGES_PER_STEP]
    km_ref = refs[PAGES_PER_STEP + 1]
    step = pl.program_id(1)
    page = k_refs[0].shape[1]
    pages_per_block = MOBA_BLOCK // page
    lane = lax.broadcasted_iota(jnp.int32, km_ref.shape, 1)

    @pl.when(step == 0)
    def _():
        km_ref[...] = jnp.zeros(km_ref.shape, F32)

    km = km_ref[...]
    for blk in range(PAGES_PER_STEP // pages_per_block):
        total = k_refs[blk * pages_per_block][...]
        for i in range(1, pages_per_block):
            total = total + k_refs[blk * pages_per_block + i][...]
        t_hi = total.astype(BF16)
        rest = total - t_hi.astype(F32)
        t_mid = rest.astype(BF16)
        t_lo = (rest - t_mid.astype(F32)).astype(BF16)
        ones = jnp.ones((page, LANES), BF16)
        mean = (_dot(t_hi, ones) + _dot(t_mid, ones) + _dot(t_lo, ones)) * (1.0 / MOBA_BLOCK)
        km = jnp.where(lane == step * (PAGES_PER_STEP // pages_per_block) + blk, mean, km)
    km_ref[...] = km

    @pl.when(step == pl.num_programs(1) - 1)
    def _():
        q = q_ref[0]
        rows = q.shape[0]
        km_hi, km_lo = _split(km)
        gate = _dot(q, km_hi) + _dot(q, km_lo)
        blk_lane = lax.broadcasted_iota(jnp.int32, (rows, LANES), 1)
        lane_f = blk_lane.astype(F32)
        gate = jnp.where(blk_lane < n_blocks, gate, -jnp.inf)
        chosen = jnp.zeros((rows, LANES), F32)
        for _k in range(min(MOBA_TOPK, n_blocks)):
            best = jnp.max(gate, axis=1, keepdims=True)
            first = jnp.min(jnp.where(gate == best, lane_f, float(LANES)), axis=1, keepdims=True)
            pick = lane_f == first
            chosen = jnp.where(pick, 1.0, chosen)
            gate = jnp.where(pick, -jnp.inf, gate)
        o_ref[0] = chosen


def _moba_select(q_all, cache_kt, page_table):
    b, n_pages = page_table.shape
    w, page = cache_kt.shape[1:]
    rows = q_all.shape[1]
    steps = n_pages // PAGES_PER_STEP
    n_blocks = n_pages * page // MOBA_BLOCK
    assert n_blocks <= LANES
    grid_spec = pltpu.PrefetchScalarGridSpec(
        num_scalar_prefetch=1,
        grid=(b, steps),
        in_specs=[pl.BlockSpec((1, rows, w), lambda bi, s, pt: (bi, 0, 0))] + _page_specs(n_pages, page, w),
        out_specs=pl.BlockSpec((1, rows, LANES), lambda bi, s, pt: (bi, 0, 0)),
        scratch_shapes=[pltpu.VMEM((w, LANES), F32)],
    )
    return pl.pallas_call(
        functools.partial(_moba_select_body, n_blocks=n_blocks),
        out_shape=jax.ShapeDtypeStruct((b, rows, LANES), F32),
        grid_spec=grid_spec,
        compiler_params=_params(("arbitrary", "arbitrary")),
        name="moba_select",
    )(page_table, q_all, *([cache_kt] * PAGES_PER_STEP))


def _moba_sample_body(pt_ref, q_ref, kn_ref, vn_ref, sel_ref, hm_ref, *rest, t_new):
    k_refs = rest[:PAGES_PER_STEP]
    v_refs = rest[PAGES_PER_STEP:2 * PAGES_PER_STEP]
    o_ref = rest[2 * PAGES_PER_STEP]
    m_ref, l_ref, acc_ref = rest[2 * PAGES_PER_STEP + 1:]
    step = pl.program_id(1)
    q = q_ref[0]
    rows = q.shape[0]
    page = k_refs[0].shape[1]
    lane = lax.broadcasted_iota(jnp.int32, (rows, LANES), 1)

    @pl.when(step == 0)
    def _():
        kn = kn_ref[0].astype(BF16)
        vn = vn_ref[0].astype(BF16)
        qt = lax.broadcasted_iota(jnp.int32, (rows, LANES), 0) % t_new
        ok = lane <= qt
        s = jnp.where(ok, _dot_nt(q, kn), NEG_BIG)
        m = jnp.max(s, axis=1, keepdims=True)
        p = jnp.where(ok, jnp.exp(s - m), 0.0)
        m_ref[...] = jnp.broadcast_to(m, (rows, LANES))
        l_ref[...] = jnp.broadcast_to(jnp.sum(p, axis=1, keepdims=True), (rows, LANES))
        acc_ref[...] = _dot(p.astype(BF16), vn)

    chosen = sel_ref[0]
    pages_per_block = MOBA_BLOCK // page
    scores = []
    for i in range(PAGES_PER_STEP):
        blk = (step * PAGES_PER_STEP + i) // pages_per_block
        picked = jnp.sum(jnp.where(lane == blk, chosen, 0.0), axis=1, keepdims=True)
        ok = jnp.broadcast_to(picked, (rows, page)) > 0.5
        scores.append(jnp.where(ok, _dot(q, k_refs[i][...].astype(BF16)), NEG_BIG))
    m = m_ref[:, 0:1]
    m_new = m
    for s in scores:
        m_new = jnp.maximum(m_new, jnp.max(s, axis=1, keepdims=True))
    alpha = jnp.exp(m - m_new)
    l = alpha * l_ref[:, 0:1]
    acc = alpha * acc_ref[...]
    for i, s in enumerate(scores):
        p = jnp.where(s > 0.5 * NEG_BIG, jnp.exp(s - m_new), 0.0)
        l = l + jnp.sum(p, axis=1, keepdims=True)
        acc = acc + _dot_nt(p.astype(BF16), v_refs[i][...].astype(BF16))
    m_ref[...] = jnp.broadcast_to(m_new, (rows, LANES))
    l_ref[...] = jnp.broadcast_to(l, (rows, LANES))
    acc_ref[...] = acc

    @pl.when(step == pl.num_programs(1) - 1)
    def _():
        o_ref[0] = (acc / l) * hm_ref[...]


def _moba_sample(q_all, head_mask, k_new, v_new, chosen, cache_kt, cache_vt, page_table, t_new):
    b, rows, w = q_all.shape
    n_pages = page_table.shape[1]
    page = cache_kt.shape[2]
    assert page == LANES
    kn = _pad_rows(k_new, LANES)
    vn = _pad_rows(v_new, LANES)
    specs = _page_specs(n_pages, page, w)
    per_b = lambda shp: pl.BlockSpec((1,) + shp, lambda bi, s, pt: (bi, 0, 0))
    const = lambda shp: pl.BlockSpec(shp, lambda bi, s, pt: (0, 0))
    grid_spec = pltpu.PrefetchScalarGridSpec(
        num_scalar_prefetch=1,
        grid=(b, n_pages // PAGES_PER_STEP),
        in_specs=[per_b((rows, w)), per_b((LANES, w)), per_b((LANES, w)), per_b((rows, LANES)),
                  const((rows, w))] + specs + specs,
        out_specs=per_b((rows, w)),
        scratch_shapes=[pltpu.VMEM((rows, LANES), F32), pltpu.VMEM((rows, LANES), F32),
                        pltpu.VMEM((rows, w), F32)],
    )
    o = pl.pallas_call(
        functools.partial(_moba_sample_body, t_new=t_new),
        out_shape=jax.ShapeDtypeStruct((b, rows, w), F32),
        grid_spec=grid_spec,
        compiler_params=_params(("arbitrary", "arbitrary")),
        name="moba_sample",
    )(page_table, q_all, kn, vn, chosen, head_mask, *([cache_kt] * PAGES_PER_STEP),
      *([cache_vt] * PAGES_PER_STEP))
    return _collapse_heads(o, t_new)


def _merge_body(osb_ref, omb_ref, sgsb_ref, sgmb_ref, x_ref, ga_ref, sh_ref, sc_ref, g2_ref, wsb_ref,
                wmb_ref, wout_ref, wr_ref, br_ref, x1_ref, h2_ref, eid_ref, gate_ref):
    u = (sgsb_ref[0].astype(F32) * _dot(osb_ref[0], wsb_ref[...])
         + sgmb_ref[0].astype(F32) * _dot(omb_ref[0], wmb_ref[...]))
    x1 = x_ref[0] + ga_ref[0] * _dot(u.astype(BF16), wout_ref[...])
    x1_ref[0] = x1
    h2 = _modulate(x1, g2_ref[...], sh_ref[0], sc_ref[0])
    h2_ref[0] = h2
    logits = _dot3(h2, wr_ref[...]) + br_ref[...]
    rows = logits.shape[0]
    lane = lax.broadcasted_iota(jnp.int32, (rows, LANES), 1)
    lane_f = lane.astype(F32)
    eid = jnp.zeros((rows, LANES), F32)
    top = jnp.zeros((rows, LANES), F32)
    best0 = None
    for k in range(TOP_K):
        best = jnp.max(logits, axis=1, keepdims=True)
        first = jnp.min(jnp.where(logits == best, lane_f, float(LANES)), axis=1, keepdims=True)
        logits = jnp.where(lane_f == first, -jnp.inf, logits)
        if best0 is None:
            best0 = best
        eid = jnp.where(lane == k, first, eid)
        top = jnp.where(lane == k, jnp.exp(best - best0), top)
    eid_ref[0] = eid.astype(jnp.int32)
    gate_ref[0] = top / jnp.sum(top, axis=1, keepdims=True)


def _merge(o_sb, o_mb, sg_sb, sg_mb, x, ga1, sh2, sc2, g2, w_sb16, w_mb16, w_out16, w_router_pad,
           b_router_pad, tm):
    b, t, d = x.shape
    assert d == SUBLANES * LANES
    r = ga1.shape[1]
    nt = t // tm
    tile = lambda wd: pl.BlockSpec((1, tm, wd), lambda bi, ti: (bi, ti, 0))
    const = lambda shp: pl.BlockSpec(shp, lambda bi, ti: (0,) * len(shp))
    mod = pl.BlockSpec((1, r, d), lambda bi, ti: (bi, 0, 0))
    return pl.pallas_call(
        _merge_body,
        out_shape=[jax.ShapeDtypeStruct((b, t, d), F32), jax.ShapeDtypeStruct((b, t, d), F32),
                   jax.ShapeDtypeStruct((b, t, LANES), jnp.int32), jax.ShapeDtypeStruct((b, t, LANES), F32)],
        grid=(b, nt),
        in_specs=[tile(WIDTH), tile(WIDTH), tile(d), tile(d), tile(d), mod, mod, mod, const((1, d)),
                  const((WIDTH, d)), const((WIDTH, d)), const((d, d)), const((d, LANES)), const((1, LANES))],
        out_specs=[tile(d), tile(d), tile(LANES), tile(LANES)],
        compiler_params=_params(("arbitrary", "arbitrary")),
        name="merge",
    )(o_sb, o_mb, sg_sb, sg_mb, x, ga1, sh2, sc2, g2, w_sb16, w_mb16, w_out16, w_router_pad, b_router_pad)


def _ffn_body(be_ref, nu_ref, tok_ref, tokn_ref, dst_ref, dstp_ref, h2_hbm, wgu_ref, bgu_ref, wd_ref, bd_ref,
              y_hbm, xbuf, obuf, wgu16, wd16, gsem, ssem, *, tm, ff):
    j = pl.program_id(0)
    n_used = nu_ref[0]
    cur = j % 2
    groups = tm // SUBLANES

    def gather_row(idx_ref, grp, sub, buf):
        tok = idx_ref[0, 0, grp * SUBLANES + sub]
        return pltpu.make_async_copy(h2_hbm.at[tok >> 3, :, tok & 7, :], xbuf.at[buf, grp, :, sub, :],
                                     gsem.at[buf])

    def scatter_row(idx_ref, grp, sub, buf):
        dst = idx_ref[0, 0, grp * SUBLANES + sub]
        return pltpu.make_async_copy(obuf.at[buf, grp, :, sub, :], y_hbm.at[dst >> 3, :, dst & 7, :],
                                     ssem.at[0])

    def for_each_group(fn):
        def body(grp, c):
            for sub in range(SUBLANES):
                fn(grp, sub)
            return c
        lax.fori_loop(0, groups, body, 0)

    def wait_gather(buf):
        pltpu.make_async_copy(h2_hbm.at[pl.ds(0, groups)], xbuf.at[buf], gsem.at[buf]).wait()

    def wait_scatter(buf):
        pltpu.make_async_copy(obuf.at[buf], y_hbm.at[pl.ds(0, groups)], ssem.at[0]).wait()

    @pl.when(j == 0)
    def _():
        n_slot_groups = y_hbm.shape[0] - 2 * groups
        obuf[...] = jnp.zeros(obuf.shape, F32)
        for half in range(2):
            fill = pltpu.make_async_copy(obuf.at[0], y_hbm.at[pl.ds(n_slot_groups + half * groups, groups)],
                                         ssem.at[0])
            fill.start()
            fill.wait()
        for_each_group(lambda grp, sub: gather_row(tok_ref, grp, sub, 0).start())

    @pl.when(j < n_used)
    def _():
        @pl.when(j >= 1)
        def _():
            wait_scatter(cur)

        @pl.when((j == 0) | (be_ref[j] != be_ref[jnp.maximum(j - 1, 0)]))
        def _():
            wgu16[...] = wgu_ref[0].astype(BF16)
            wd16[...] = wd_ref[0].astype(BF16)

        wait_gather(cur)
        x16 = jnp.concatenate([xbuf[cur, :, c, :, :].reshape(tm, LANES).astype(BF16)
                               for c in range(SUBLANES)], axis=1)
        y = jnp.zeros((tm, wd16.shape[1]), F32) + bd_ref[0]
        chunk = 512
        n_chunks = ff // chunk
        issue_until = [min(tm, -(-tm // (n_chunks - 1)) * (c + 1)) for c in range(n_chunks - 1)] + [tm]
        for c in range(n_chunks):
            g = _dot(x16, wgu16[:, c * chunk:(c + 1) * chunk]) + bgu_ref[0, :, c * chunk:(c + 1) * chunk]
            u = (_dot(x16, wgu16[:, ff + c * chunk:ff + (c + 1) * chunk])
                 + bgu_ref[0, :, ff + c * chunk:ff + (c + 1) * chunk])
            g = jnp.minimum(g, SWIGLU_LIMIT)
            u = jnp.clip(u, -SWIGLU_LIMIT, SWIGLU_LIMIT)
            act = (u + 1.0) * (g * jax.nn.sigmoid(SWIGLU_ALPHA * g))
            y = y + _dot(act.astype(BF16), wd16[c * chunk:(c + 1) * chunk, :])
            for r in range(issue_until[c - 1] if c else 0, issue_until[c]):
                gather_row(tokn_ref, r // SUBLANES, r % SUBLANES, 1 - cur).start(priority=0)
                scatter_row(dstp_ref, r // SUBLANES, r % SUBLANES, 1 - cur).start(priority=1)

        for c in range(SUBLANES):
            obuf[cur, :, c, :, :] = y[:, c * LANES:(c + 1) * LANES].reshape(groups, SUBLANES, LANES)

        @pl.when(j == n_used - 1)
        def _():
            wait_scatter(1 - cur)
            wait_gather(1 - cur)
            for_each_group(lambda grp, sub: scatter_row(dst_ref, grp, sub, cur).start())
            wait_scatter(cur)


def _row_groups(x):
    rows, d = x.shape
    return x.reshape(rows // SUBLANES, SUBLANES, d // LANES, LANES).transpose(0, 2, 1, 3)


def _expert_ffn(h2_rows, row_tok, row_dst, row_dst_prev, block_e, n_used, w_gu, b_gu, w_down, b_down,
                n_out_rows):
    n_blocks, _, tm = row_tok.shape
    e, d, ff2 = w_gu.shape
    ff = ff2 // 2
    assert d == SUBLANES * LANES
    groups = tm // SUBLANES

    def blk(j, be, nu):
        return (jnp.minimum(j, nu[0] - 1), 0, 0)

    def blk_next(j, be, nu):
        return (jnp.minimum(j + 1, nu[0] - 1), 0, 0)

    smem = lambda index: pl.BlockSpec((1, 1, tm), index, memory_space=pltpu.SMEM)
    per_e = lambda shp: pl.BlockSpec((1,) + shp, lambda j, be, nu: (be[j], 0, 0))
    grid_spec = pltpu.PrefetchScalarGridSpec(
        num_scalar_prefetch=2,
        grid=(n_blocks,),
        in_specs=[smem(blk), smem(blk_next), smem(blk), smem(blk), pl.BlockSpec(memory_space=pl.ANY),
                  per_e((d, ff2)), per_e((1, ff2)), per_e((ff, d)), per_e((1, d))],
        out_specs=pl.BlockSpec(memory_space=pl.ANY),
        scratch_shapes=[pltpu.VMEM((2, groups, SUBLANES, SUBLANES, LANES), F32),
                        pltpu.VMEM((2, groups, SUBLANES, SUBLANES, LANES), F32),
                        pltpu.VMEM((d, ff2), BF16), pltpu.VMEM((ff, d), BF16),
                        pltpu.SemaphoreType.DMA((2,)), pltpu.SemaphoreType.DMA((1,))],
    )
    y = pl.pallas_call(
        functools.partial(_ffn_body, tm=tm, ff=ff),
        out_shape=jax.ShapeDtypeStruct((n_out_rows // SUBLANES, SUBLANES, SUBLANES, LANES), F32),
        grid_spec=grid_spec,
        compiler_params=_params(("arbitrary",), vmem=56 * 1024 * 1024),
        name="expert_ffn",
    )(block_e, n_used, row_tok, row_tok, row_dst, row_dst_prev, _row_groups(h2_rows), w_gu,
      b_gu.reshape(e, 1, ff2), w_down, b_down.reshape(e, 1, d))
    return y.transpose(0, 2, 1, 3).reshape(n_out_rows, d)


def _route(eid, n_tokens):
    tile = EXPERT_TILE
    nk = n_tokens * TOP_K
    n_blocks = (nk + N_EXPERTS * (tile - 1) + tile - 1) // tile
    rows = n_blocks * tile
    onehot = (eid[:, :, None] == jnp.arange(N_EXPERTS, dtype=jnp.int32)[None, None, :]).astype(jnp.int32)
    per_token = onehot.sum(axis=1)
    before = jnp.cumsum(per_token, axis=0) - per_token
    counts = per_token.sum(axis=0)
    padded = (counts + tile - 1) // tile * tile
    pend = jnp.cumsum(padded)
    pstart = pend - padded
    dest = ((before + pstart[None, :])[:, None, :] * onehot).sum(axis=2).reshape(-1).astype(jnp.int32)
    pos = jnp.arange(rows, dtype=jnp.int32)
    row_flat = jnp.full((rows,), -1, jnp.int32).at[dest].set(
        jnp.arange(nk, dtype=jnp.int32), unique_indices=True)
    is_pad = row_flat < 0
    dump = nk + ((pos // tile) % 2) * tile + pos % tile
    row_tok = jnp.where(is_pad, 0, row_flat // TOP_K).astype(jnp.int32)
    row_dst = jnp.where(is_pad, dump, (row_flat % TOP_K) * n_tokens + row_flat // TOP_K).astype(jnp.int32)
    n_used = (pend[-1] // tile).astype(jnp.int32)
    blk_start = jnp.minimum(jnp.arange(n_blocks), n_used - 1) * tile
    block_e = jnp.minimum((pend[None, :] <= blk_start[:, None]).sum(axis=1), N_EXPERTS - 1).astype(jnp.int32)
    row_dst = row_dst.reshape(n_blocks, 1, tile)
    first = (nk + tile + jnp.arange(tile, dtype=jnp.int32)).reshape(1, 1, tile)
    row_dst_prev = jnp.concatenate([first, row_dst[:-1]], axis=0)
    return (row_tok.reshape(n_blocks, 1, tile), row_dst, row_dst_prev, block_e, n_used.reshape(1),
            nk + 2 * tile)


def _combine_body(y0_ref, y1_ref, y2_ref, y3_ref, gate_ref, x1_ref, ga_ref, o_ref):
    gate = gate_ref[0]
    ffn = None
    for k, y_ref in enumerate((y0_ref, y1_ref, y2_ref, y3_ref)):
        part = gate[:, k:k + 1] * y_ref[...]
        ffn = part if ffn is None else ffn + part
    o_ref[0] = x1_ref[0] + ga_ref[0] * ffn


def _combine(y, gate, x1, ga2, tm, first_token, n_tokens):
    b, t, d = x1.shape
    r = ga2.shape[1]
    nt = t // tm
    assert first_token % tm == 0 and n_tokens % tm == 0
    tile = lambda wd: pl.BlockSpec((1, tm, wd), lambda bi, ti: (bi, ti, 0))

    def y_spec(k):
        base = (k * n_tokens + first_token) // tm
        return pl.BlockSpec((tm, d), lambda bi, ti: (base + bi * nt + ti, 0))

    return pl.pallas_call(
        _combine_body,
        out_shape=jax.ShapeDtypeStruct((b, t, d), F32),
        grid=(b, nt),
        in_specs=[y_spec(k) for k in range(TOP_K)]
        + [tile(LANES), tile(d), pl.BlockSpec((1, r if r == 1 else tm, d),
                                              lambda bi, ti: (bi, 0, 0) if r == 1 else (bi, ti, 0))],
        out_specs=tile(d),
        compiler_params=_params(("arbitrary", "arbitrary")),
        name="combine",
    )(y, y, y, y, gate, x1, ga2)


def _head_gain(g):
    return jnp.tile(g.astype(F32), N_HEADS).reshape(1, WIDTH)


def _mods(mod, d):
    return [mod[:, None, i * d:(i + 1) * d] for i in range(N_MOD)]


def kernel(x_prompt, x_sample, c_prompt, c_sample, cache_sb_k, cache_sb_v, cache_mb_k, cache_mb_v, page_table,
           w_ada, b_ada, g_norm1, g_norm2, w_in, q_norm_g, k_norm_g, w_br_sb, w_br_mb, w_out, w_router,
           b_router, w_gu, b_gu, w_down, b_down):
    depth = w_ada.shape[0]
    assert depth == 1
    bp, seq, d = x_prompt.shape
    bs, t_new, _ = x_sample.shape
    page = cache_sb_k.shape[2]
    past = page_table.shape[1] * page
    n_prompt = bp * seq
    n_sample = bs * t_new
    lyr = 0

    mod = _adaln(jnp.concatenate([c_prompt, c_sample], axis=0), w_ada[lyr], b_ada[lyr])
    sh1p, sc1p, ga1p, sh2p, sc2p, ga2p = _mods(mod[:bp], d)
    per_row = lambda m: jnp.repeat(m, t_new, axis=0).reshape(1, n_sample, d)
    sh1s, sc1s, ga1s, sh2s, sc2s, ga2s = [per_row(m[:, 0]) for m in _mods(mod[bp:], d)]

    w_in16 = w_in[lyr].astype(BF16)
    g1 = g_norm1[lyr].reshape(1, d)
    g2 = g_norm2[lyr].reshape(1, d)
    qg = _head_gain(q_norm_g[lyr])
    kg = _head_gain(k_norm_g[lyr])
    head_of = np.arange(WIDTH) // HEAD_DIM
    gmat = jnp.asarray((head_of[:, None] == head_of[None, :]).astype(np.float32) / HEAD_DIM, dtype=BF16)

    pos_p = jnp.arange(seq, dtype=jnp.int32)
    (qsb, ksb, vsb, ksb16, vsb16, qmb, kmb, vmb, kmb16, _, sgsb, sgmb, kmean, vmbt16) = _inproj(
        x_prompt, sh1p, sc1p, g1, w_in16, pos_p, qg, kg, gmat, ROW_TILE, True)
    o_sb_p = _sb_prompt(qsb, ksb16, vsb16)
    o_mb_p = _moba_prompt(qmb, kmb16, vmbt16, kmean)

    xs = x_sample.reshape(1, n_sample, d)
    pos_s = past + jnp.arange(n_sample, dtype=jnp.int32) % t_new
    (qsb_s, ksb_s, vsb_s, _, _, qmb_s, kmb_s, vmb_s, _, _, sgsb_s, sgmb_s) = _inproj(
        xs, sh1s, sc1s, g1, w_in16, pos_s, qg, kg, gmat, n_sample, False)
    by_batch = lambda a: a.reshape(bs, t_new, WIDTH)
    o_sb_s = _sb_sample(by_batch(qsb_s), by_batch(ksb_s), by_batch(vsb_s), _pages_t(cache_sb_k, lyr),
                        _pages_t(cache_sb_v, lyr), page_table)
    mb_kt = _pages_t(cache_mb_k, lyr)
    q_all_mb, head_mask = _expand_heads(by_batch(qmb_s))
    chosen = _moba_select(q_all_mb, mb_kt, page_table)
    o_mb_s = _moba_sample(q_all_mb, head_mask, by_batch(kmb_s), by_batch(vmb_s), chosen, mb_kt,
                          _pages_t(cache_mb_v, lyr), page_table, t_new)
    o_sb_s = o_sb_s.reshape(1, n_sample, WIDTH).astype(BF16)
    o_mb_s = o_mb_s.reshape(1, n_sample, WIDTH).astype(BF16)

    w_sb16 = w_br_sb[lyr].astype(BF16)
    w_mb16 = w_br_mb[lyr].astype(BF16)
    w_out16 = w_out[lyr].astype(BF16)
    w_router_pad = jnp.pad(w_router[lyr], ((0, 0), (0, LANES - N_EXPERTS)))
    b_router_pad = jnp.concatenate([b_router[lyr].astype(F32),
                                    jnp.full((LANES - N_EXPERTS,), -jnp.inf, F32)]).reshape(1, LANES)
    x1p, h2p, eidp, gatep = _merge(o_sb_p, o_mb_p, sgsb, sgmb, x_prompt, ga1p, sh2p, sc2p, g2, w_sb16,
                                   w_mb16, w_out16, w_router_pad, b_router_pad, ROW_TILE)
    x1s, h2s, eids, gates = _merge(o_sb_s, o_mb_s, sgsb_s, sgmb_s, xs, ga1s, sh2s, sc2s, g2, w_sb16,
                                   w_mb16, w_out16, w_router_pad, b_router_pad, n_sample)

    n_tokens = n_prompt + n_sample
    h2_rows = jnp.concatenate([h2p.reshape(n_prompt, d), h2s.reshape(n_sample, d)])
    eid = jnp.concatenate([eidp.reshape(n_prompt, LANES), eids.reshape(n_sample, LANES)])[:, :TOP_K]
    row_tok, row_dst, row_dst_prev, block_e, n_used, n_out_rows = _route(eid, n_tokens)
    y_rows = _expert_ffn(h2_rows, row_tok, row_dst, row_dst_prev, block_e, n_used, w_gu[lyr], b_gu[lyr],
                         w_down[lyr], b_down[lyr], n_out_rows)
    y_prompt = _combine(y_rows, gatep, x1p, ga2p, COMBINE_TILE, 0, n_tokens)
    y_sample = _combine(y_rows, gates, x1s, ga2s, COMBINE_TILE, n_prompt, n_tokens)

    heads = lambda a, b_, t_: a.reshape(1, b_, t_, N_HEADS, HEAD_DIM)
    return (y_prompt, y_sample.reshape(bs, t_new, d),
            heads(ksb, bp, seq), heads(vsb, bp, seq), heads(kmb, bp, seq), heads(vmb, bp, seq),
            heads(ksb_s, bs, t_new), heads(vsb_s, bs, t_new), heads(kmb_s, bs, t_new), heads(vmb_s, bs, t_new))
```

```python
import functools

import numpy as np
import jax
import jax.numpy as jnp
from jax import lax
from jax.experimental import pallas as pl
from jax.experimental.pallas import tpu as pltpu

HEAD_DIM = 64
N_HEADS = 8
WIDTH = N_HEADS * HEAD_DIM
MOBA_BLOCK = 256
MOBA_TOPK = 3
ROPE_THETA = 10000.0
N_EXPERTS = 32
TOP_K = 4
SWIGLU_LIMIT = 7.0
SWIGLU_ALPHA = 1.702
NORM_EPS = 1e-6
N_MOD = 6
ATTN_SCALE = HEAD_DIM ** -0.5

LANES = 128
SUBLANES = 8
ROW_TILE = 256
ATTN_TILE = 256
EXPERT_TILE = 512
COMBINE_TILE = 128
PAGES_PER_STEP = 16
NEG_BIG = -1e30
STICK_FLOOR = -104.0
VMEM_LIMIT = 48 * 1024 * 1024

F32 = jnp.float32
BF16 = jnp.bfloat16


def _dot(a, b):
    return jnp.dot(a, b, preferred_element_type=F32)


def _dot_nt(a, b):
    return lax.dot_general(a, b, (((1,), (1,)), ((), ())), preferred_element_type=F32)


def _split(x):
    hi = x.astype(BF16)
    lo = (x - hi.astype(F32)).astype(BF16)
    return hi, lo


def _dot3(a, b):
    a_hi, a_lo = _split(a)
    b_hi, b_lo = _split(b)
    return _dot(a_hi, b_hi) + _dot(a_lo, b_hi) + _dot(a_hi, b_lo)


def _params(sem, vmem=VMEM_LIMIT):
    return pltpu.CompilerParams(dimension_semantics=sem, vmem_limit_bytes=vmem)


def _adaln_body(c_ref, w_ref, b_ref, o_ref):
    c = c_ref[...]
    s = c * jax.nn.sigmoid(c)
    o_ref[...] = _dot3(s, w_ref[...]) + b_ref[...]


def _adaln(c, w_ada, b_ada):
    rows, d = c.shape
    cols = w_ada.shape[1]
    tn = 1024
    return pl.pallas_call(
        _adaln_body,
        out_shape=jax.ShapeDtypeStruct((rows, cols), F32),
        grid=(cols // tn,),
        in_specs=[pl.BlockSpec((rows, d), lambda j: (0, 0)),
                  pl.BlockSpec((d, tn), lambda j: (0, j)),
                  pl.BlockSpec((1, tn), lambda j: (0, j))],
        out_specs=pl.BlockSpec((rows, tn), lambda j: (0, j)),
        compiler_params=_params(("arbitrary",)),
        name="adaln",
    )(c, w_ada, b_ada.reshape(1, cols))


def _modulate(x, g, shift, scale):
    ms = jnp.mean(x * x, axis=-1, keepdims=True)
    return (x * lax.rsqrt(ms + NORM_EPS) * g) * (1.0 + scale) + shift


def _qk_norm_rope(p, g, gmat, cos, sin_signed):
    ms = _dot((p * p).astype(BF16), gmat)
    y = p * lax.rsqrt(ms + NORM_EPS) * g
    lane = lax.broadcasted_iota(jnp.int32, (p.shape[0], LANES), 1)
    first_half = (lane & (HEAD_DIM // 2)) == 0
    outs = []
    for c in range(WIDTH // LANES):
        yc = y[:, c * LANES:(c + 1) * LANES]
        partner = jnp.where(first_half, pltpu.roll(yc, LANES - HEAD_DIM // 2, 1),
                            pltpu.roll(yc, HEAD_DIM // 2, 1))
        outs.append(yc * cos + partner * sin_signed)
    return jnp.concatenate(outs, axis=1)


def _inproj_body(emit_kmean, x_ref, sh_ref, sc_ref, g1_ref, w_ref, cos_ref, sin_ref, qg_ref, kg_ref,
                 gmat_ref, qsb_ref, ksb_ref, vsb_ref, ksb16_ref, vsb16_ref, qmb_ref, kmb_ref, vmb_ref,
                 kmb16_ref, vmb16_ref, sgsb_ref, sgmb_ref, *rest):
    h = _modulate(x_ref[0], g1_ref[...], sh_ref[0], sc_ref[0])
    h16 = h.astype(BF16)

    def proj(c0, width):
        return _dot(h16, w_ref[:, c0:c0 + width])

    w = WIDTH
    qsb_ref[0] = (proj(0, w) * ATTN_SCALE).astype(BF16)
    ksb = proj(w, w)
    ksb_ref[0] = ksb
    ksb16_ref[0] = ksb.astype(BF16)
    vsb = proj(2 * w, w)
    vsb_ref[0] = vsb
    vsb16_ref[0] = vsb.astype(BF16)
    cos = cos_ref[...]
    sin = sin_ref[...]
    gmat = gmat_ref[...]
    qmb = _qk_norm_rope(proj(3 * w, w), qg_ref[...], gmat, cos, sin)
    qmb_ref[0] = (qmb * ATTN_SCALE).astype(BF16)
    kmb = _qk_norm_rope(proj(4 * w, w), kg_ref[...], gmat, cos, sin)
    kmb_ref[0] = kmb
    kmb16_ref[0] = kmb.astype(BF16)
    vmb = proj(5 * w, w)
    vmb_ref[0] = vmb
    vmb16_ref[0] = vmb.astype(BF16)
    d = x_ref.shape[2]
    sgsb_ref[0] = jax.nn.sigmoid(proj(6 * w, d)).astype(BF16)
    sgmb_ref[0] = jax.nn.sigmoid(proj(6 * w + d, d)).astype(BF16)
    if emit_kmean:
        kmean_ref, vmbt_ref = rest
        ti = pl.program_id(1)
        kmean_ref[0, pl.ds(ti, 1), :] = jnp.mean(kmb, axis=0, keepdims=True)
        vmbt_ref[0, 0] = vmb.T.astype(BF16)


def _rope_tables(pos):
    inv_freq = ROPE_THETA ** (-jnp.arange(0, HEAD_DIM, 2, dtype=F32) / HEAD_DIM)
    ang = pos.astype(F32)[:, None] * inv_freq[None, :]
    cos = jnp.tile(jnp.cos(ang), (1, LANES // (HEAD_DIM // 2)))
    sin = jnp.sin(ang)
    sin_signed = jnp.tile(jnp.concatenate([-sin, sin], axis=1), (1, LANES // HEAD_DIM))
    return cos, sin_signed


def _inproj(x, shift, scale, g1, w_in16, pos, qg, kg, gmat, tm, emit_kmean):
    b, t, d = x.shape
    r = shift.shape[1]
    cols = w_in16.shape[1]
    nt = t // tm
    cos, sin_signed = _rope_tables(pos)
    row = lambda dt, wd: jax.ShapeDtypeStruct((b, t, wd), dt)
    tile = lambda wd: pl.BlockSpec((1, tm, wd), lambda bi, ti: (bi, ti, 0))
    const = lambda shp: pl.BlockSpec(shp, lambda bi, ti: (0,) * len(shp))
    mod = pl.BlockSpec((1, r, d), lambda bi, ti: (bi, 0, 0))
    out_shape = [row(BF16, WIDTH), row(F32, WIDTH), row(F32, WIDTH), row(BF16, WIDTH), row(BF16, WIDTH),
                 row(BF16, WIDTH), row(F32, WIDTH), row(F32, WIDTH), row(BF16, WIDTH), row(BF16, WIDTH),
                 row(BF16, d), row(BF16, d)]
    out_specs = [tile(WIDTH)] * 10 + [tile(d)] * 2
    if emit_kmean:
        assert tm == MOBA_BLOCK
        out_shape.append(jax.ShapeDtypeStruct((b, nt, WIDTH), F32))
        out_specs.append(pl.BlockSpec((1, nt, WIDTH), lambda bi, ti: (bi, 0, 0)))
        out_shape.append(jax.ShapeDtypeStruct((b, nt, WIDTH, tm), BF16))
        out_specs.append(pl.BlockSpec((1, 1, WIDTH, tm), lambda bi, ti: (bi, ti, 0, 0)))
    return pl.pallas_call(
        functools.partial(_inproj_body, emit_kmean),
        out_shape=out_shape,
        grid=(b, nt),
        in_specs=[tile(d), mod, mod, const((1, d)), const((d, cols)),
                  pl.BlockSpec((tm, LANES), lambda bi, ti: (ti, 0)),
                  pl.BlockSpec((tm, LANES), lambda bi, ti: (ti, 0)),
                  const((1, WIDTH)), const((1, WIDTH)), const((WIDTH, WIDTH))],
        out_specs=out_specs,
        compiler_params=_params(("arbitrary", "arbitrary")),
        name="inproj_kmean" if emit_kmean else "inproj",
    )(x, shift, scale, g1, w_in16, cos, sin_signed, qg, kg, gmat)


def _stick_tile(z, allowed, tri, carry):
    sp = jnp.maximum(z, 0.0) + jnp.log(1.0 + jnp.exp(-jnp.abs(z)))
    log_keep = -sp
    if allowed is not None:
        log_keep = jnp.where(allowed, log_keep, 0.0)
    lk_hi, lk_lo = _split(log_keep)
    later = _dot(lk_hi, tri) + _dot(lk_lo, tri)
    ones = jnp.ones((z.shape[1], LANES), BF16)
    total = _dot(lk_hi, ones) + _dot(lk_lo, ones)
    reps = z.shape[1] // LANES
    carry_wide = carry if reps == 1 else jnp.concatenate([carry] * reps, axis=1)
    a = jnp.exp((z - sp) + later + carry_wide)
    if allowed is not None:
        a = jnp.where(allowed, a, 0.0)
    return a, carry + total


def _head_masks(rows):
    lane = lax.broadcasted_iota(jnp.int32, (rows, LANES), 1)
    return lane < HEAD_DIM


def _sb_prompt_body(q_ref, k_ref, v_ref, tri_ref, o_ref, *, tile):
    qi = pl.program_id(2)
    q = q_ref[0]
    tri = tri_ref[...]
    low = _head_masks(tile)
    row = lax.broadcasted_iota(jnp.int32, (tile, tile), 0)
    col = lax.broadcasted_iota(jnp.int32, (tile, tile), 1)
    diag_allowed = col < row
    zeros16 = jnp.zeros_like(q)
    qms = (jnp.where(low, q, zeros16), jnp.where(low, zeros16, q))

    def logits(kb):
        start = pl.multiple_of(kb * tile, tile)
        k_blk = k_ref[0, pl.ds(start, tile), :]
        return tuple(_dot_nt(qms[h], k_blk) for h in range(2))

    def key_tile(kb, z, state, allowed):
        start = pl.multiple_of(kb * tile, tile)
        v_blk = v_ref[0, pl.ds(start, tile), :]
        out = []
        for h in range(2):
            carry, acc = state[h]
            a, carry = _stick_tile(z[h], allowed, tri, carry)
            out.append((carry, acc + _dot(a.astype(BF16), v_blk)))
        return tuple(out)

    zero = jnp.zeros((tile, LANES), F32)
    z_next = logits(jnp.maximum(qi - 1, 0))
    state = key_tile(qi, logits(qi), ((zero, zero), (zero, zero)), diag_allowed)

    def live(st):
        return jnp.maximum(jnp.max(st[0][0]), jnp.max(st[1][0])) > STICK_FLOOR

    def cond(c):
        return (c[0] < qi) & live(c[2])

    def body(c):
        i, z, st = c
        kb = qi - 1 - i
        z_after = logits(jnp.maximum(kb - 1, 0))
        return i + 1, z_after, key_tile(kb, z, st, None)

    _, _, state = lax.while_loop(cond, body, (jnp.int32(0), z_next, state))
    o_ref[0] = jnp.where(low, state[0][1], state[1][1]).astype(o_ref.dtype)


def _tri(n):
    j = np.arange(n)[:, None]
    s = np.arange(n)[None, :]
    return jnp.asarray((j > s).astype(np.float32), dtype=BF16)


def _sb_prompt(q16, k16, v16):
    b, s, w = q16.shape
    tile = ATTN_TILE
    pairs = w // LANES
    return pl.pallas_call(
        functools.partial(_sb_prompt_body, tile=tile),
        out_shape=jax.ShapeDtypeStruct((b, s, w), BF16),
        grid=(b, pairs, s // tile),
        in_specs=[pl.BlockSpec((1, tile, LANES), lambda bi, p, qi: (bi, qi, p)),
                  pl.BlockSpec((1, s, LANES), lambda bi, p, qi: (bi, 0, p)),
                  pl.BlockSpec((1, s, LANES), lambda bi, p, qi: (bi, 0, p)),
                  pl.BlockSpec((tile, tile), lambda bi, p, qi: (0, 0))],
        out_specs=pl.BlockSpec((1, tile, LANES), lambda bi, p, qi: (bi, qi, p)),
        compiler_params=_params(("arbitrary",) * 3),
        name="sb_prompt",
    )(q16, k16, v16, _tri(tile))


def _moba_prompt_body(q_ref, k_ref, vt_ref, km_ref, o_ref, sel_ref, *, tile, nb):
    own = pl.program_id(2)
    q = q_ref[0]
    assert nb <= SUBLANES
    km_pad = jnp.concatenate([km_ref[0], jnp.zeros((2 * SUBLANES - nb, LANES), F32)], axis=0)
    km_hi, km_lo = _split(km_pad)
    low = _head_masks(tile)
    zeros16 = jnp.zeros_like(q)
    qms = (jnp.where(low, q, zeros16), jnp.where(low, zeros16, q))
    blk = lax.broadcasted_iota(jnp.int32, (SUBLANES, tile), 0)
    past = blk < own
    for h in range(2):
        gate = (_dot_nt(km_hi, qms[h]) + _dot_nt(km_lo, qms[h]))[:SUBLANES]
        gate = jnp.where(past, gate, -jnp.inf)
        ahead = jnp.zeros((SUBLANES, tile), F32)
        for r in range(1, SUBLANES):
            other = pltpu.roll(gate, r, 0)
            other_blk = (blk - r) & (SUBLANES - 1)
            ahead = ahead + jnp.where(other > gate, 1.0,
                                      jnp.where((other == gate) & (other_blk < blk), 1.0, 0.0))
        sel_ref[h] = jnp.where(past, jnp.where(ahead < MOBA_TOPK, 1.0, 0.0), 0.0)

    key_i = lax.broadcasted_iota(jnp.int32, (tile, tile), 0)
    qry_i = lax.broadcasted_iota(jnp.int32, (tile, tile), 1)
    causal = key_i <= qry_i

    def logits(kb):
        start = pl.multiple_of(kb * tile, tile)
        k_blk = k_ref[0, pl.ds(start, tile), :]
        return tuple(_dot_nt(k_blk, qms[h]) for h in range(2))

    def key_tile(kb, z, state, allowed):
        vt_blk = vt_ref[0, kb]
        out = []
        for h in range(2):
            m, l, acc = state[h]
            ok = allowed
            if ok is None:
                ok = jnp.broadcast_to(sel_ref[h, pl.ds(kb, 1), :], (tile, tile)) > 0.5
            s = jnp.where(ok, z[h], NEG_BIG)
            m_new = jnp.maximum(m, jnp.max(s, axis=0, keepdims=True))
            p = jnp.where(ok, jnp.exp(s - m_new), 0.0)
            alpha = jnp.exp(m - m_new)
            l = alpha * l + jnp.sum(p, axis=0, keepdims=True)
            acc = alpha * acc + _dot(vt_blk, p.astype(BF16))
            out.append((m_new, l, acc))
        return tuple(out)

    m0 = jnp.full((1, tile), NEG_BIG, F32)
    l0 = jnp.zeros((1, tile), F32)
    a0 = jnp.zeros((LANES, tile), F32)
    z_first = logits(0)
    state = key_tile(own, logits(own), ((m0, l0, a0), (m0, l0, a0)), causal)

    def body(kb, c):
        z_after = logits(jnp.minimum(kb + 1, nb - 1))
        return z_after, key_tile(kb, c[0], c[1], None)

    _, state = lax.fori_loop(0, own, body, (z_first, state))
    outs = [state[h][2] / state[h][1] for h in range(2)]
    dim = lax.broadcasted_iota(jnp.int32, (LANES, tile), 0)
    o_ref[0] = jnp.where(dim < HEAD_DIM, outs[0], outs[1]).T.astype(o_ref.dtype)


def _moba_prompt(q16, k16, vt16, kmean):
    b, s, w = q16.shape
    tile = MOBA_BLOCK
    nb = s // tile
    pairs = w // LANES
    return pl.pallas_call(
        functools.partial(_moba_prompt_body, tile=tile, nb=nb),
        out_shape=jax.ShapeDtypeStruct((b, s, w), BF16),
        grid=(b, pairs, nb),
        in_specs=[pl.BlockSpec((1, tile, LANES), lambda bi, p, qi: (bi, qi, p)),
                  pl.BlockSpec((1, s, LANES), lambda bi, p, qi: (bi, 0, p)),
                  pl.BlockSpec((1, nb, LANES, tile), lambda bi, p, qi: (bi, 0, p, 0)),
                  pl.BlockSpec((1, nb, LANES), lambda bi, p, qi: (bi, 0, p))],
        out_specs=pl.BlockSpec((1, tile, LANES), lambda bi, p, qi: (bi, qi, p)),
        scratch_shapes=[pltpu.VMEM((2, SUBLANES, tile), F32)],
        compiler_params=_params(("arbitrary",) * 3),
        name="moba_prompt",
    )(q16, k16, vt16, kmean)


def _pages_t(cache, layer):
    n_phys, page = cache.shape[1:3]
    return jnp.transpose(cache[layer], (0, 2, 3, 1)).reshape(n_phys, WIDTH, page)


def _page_specs(n_pages, page, width):
    specs = []
    for i in range(PAGES_PER_STEP):
        def index(bi, s, pt, i=i):
            return (pt[bi, s * PAGES_PER_STEP + i], 0, 0)
        specs.append(pl.BlockSpec((None, width, page), index))
    return specs


def _expand_heads(q):
    b, t, w = q.shape
    head_of_lane = jnp.arange(w) // HEAD_DIM
    mask = (head_of_lane[None, :] == jnp.arange(N_HEADS)[:, None]).astype(q.dtype)
    return (q[:, None, :, :] * mask[None, :, None, :]).reshape(b, N_HEADS * t, w), \
        jnp.repeat(mask, t, axis=0).astype(F32)


def _collapse_heads(o, t):
    b, _, w = o.shape
    return o.reshape(b, N_HEADS, t, w).sum(axis=1)


def _pad_rows(x, rows):
    return jnp.pad(x, ((0, 0), (0, rows - x.shape[1]), (0, 0)))


def _sb_sample_body(pt_ref, q_ref, kn_ref, vn_ref, tri_ref, hm_ref, kc_hbm, vc_hbm, o_ref, kbuf, vbuf, sem,
                    *, n_pages, t_new):
    b = pl.program_id(0)
    q = q_ref[0]
    rows = q.shape[0]
    tri = tri_ref[...]

    def page_copies(bi, p, slot):
        phys = pt_ref[bi, p]
        return (pltpu.make_async_copy(kc_hbm.at[phys], kbuf.at[slot], sem.at[0, slot]),
                pltpu.make_async_copy(vc_hbm.at[phys], vbuf.at[slot], sem.at[1, slot]))

    def start(bi, p, slot):
        for c in page_copies(bi, p, slot):
            c.start()

    def wait(bi, p, slot):
        for c in page_copies(bi, p, slot):
            c.wait()

    def slot_of(p):
        return (n_pages - 1 - p) % 2

    @pl.when(b == 0)
    def _():
        start(b, n_pages - 1, 0)

    kn = kn_ref[0].astype(BF16)
    vn = vn_ref[0].astype(BF16)
    key = lax.broadcasted_iota(jnp.int32, (rows, LANES), 1)
    qt = lax.broadcasted_iota(jnp.int32, (rows, LANES), 0) % t_new
    a, carry = _stick_tile(_dot_nt(q, kn), key < qt, tri, jnp.zeros((rows, LANES), F32))
    acc = _dot(a.astype(BF16), vn)

    def cond(c):
        return (c[0] >= 0) & (jnp.max(c[1]) > STICK_FLOOR)

    def body(c):
        p, carry, acc = c
        slot = slot_of(p)
        wait(b, p, slot)

        @pl.when(p >= 1)
        def _():
            start(b, p - 1, 1 - slot)

        k_t = kbuf[slot].astype(BF16)
        v_t = vbuf[slot].astype(BF16)
        a, carry = _stick_tile(_dot(q, k_t), None, tri, carry)
        return p - 1, carry, acc + _dot_nt(a.astype(BF16), v_t)

    p, carry, acc = lax.while_loop(cond, body, (jnp.int32(n_pages - 1), carry, acc))

    @pl.when(p >= 0)
    def _():
        wait(b, p, slot_of(p))

    @pl.when(b + 1 < pl.num_programs(0))
    def _():
        start(b + 1, n_pages - 1, 0)

    o_ref[0] = acc * hm_ref[...]


def _sb_sample(q16, k_new, v_new, cache_kt, cache_vt, page_table):
    b, t, w = q16.shape
    n_pages = page_table.shape[1]
    page = cache_kt.shape[2]
    q_all, head_mask = _expand_heads(q16)
    rows = q_all.shape[1]
    assert page == LANES
    kn = _pad_rows(k_new, LANES)
    vn = _pad_rows(v_new, LANES)
    per_b = lambda shp: pl.BlockSpec((1,) + shp, lambda bi, pt: (bi, 0, 0))
    const = lambda shp: pl.BlockSpec(shp, lambda bi, pt: (0, 0))
    grid_spec = pltpu.PrefetchScalarGridSpec(
        num_scalar_prefetch=1,
        grid=(b,),
        in_specs=[per_b((rows, w)), per_b((LANES, w)), per_b((LANES, w)), const((page, page)),
                  const((rows, w)), pl.BlockSpec(memory_space=pl.ANY), pl.BlockSpec(memory_space=pl.ANY)],
        out_specs=per_b((rows, w)),
        scratch_shapes=[pltpu.VMEM((2, w, page), F32), pltpu.VMEM((2, w, page), F32),
                        pltpu.SemaphoreType.DMA((2, 2))],
    )
    o = pl.pallas_call(
        functools.partial(_sb_sample_body, n_pages=n_pages, t_new=t),
        out_shape=jax.ShapeDtypeStruct((b, rows, w), F32),
        grid_spec=grid_spec,
        compiler_params=_params(("arbitrary",)),
        name="sb_sample",
    )(page_table, q_all, kn, vn, _tri(page), head_mask, cache_kt, cache_vt)
    return _collapse_heads(o, t)


def _moba_select_body(pt_ref, q_ref, *refs, n_blocks):
    k_refs = refs[:PAGES_PER_STEP]
    o_ref = refs[PAGES_PER_STEP]
    km_ref = refs[PAGES_PER_STEP + 1]
    step = pl.program_id(1)
    page = k_refs[0].shape[1]
    pages_per_block = MOBA_BLOCK // page
    lane = lax.broadcasted_iota(jnp.int32, km_ref.shape, 1)

    @pl.when(step == 0)
    def _():
        km_ref[...] = jnp.zeros(km_ref.shape, F32)

    km = km_ref[...]
    for blk in range(PAGES_PER_STEP // pages_per_block):
        total = k_refs[blk * pages_per_block][...]
        for i in range(1, pages_per_block):
            total = total + k_refs[blk * pages_per_block + i][...]
        t_hi = total.astype(BF16)
        rest = total - t_hi.astype(F32)
        t_mid = rest.astype(BF16)
        t_lo = (rest - t_mid.astype(F32)).astype(BF16)
        ones = jnp.ones((page, LANES), BF16)
        mean = (_dot(t_hi, ones) + _dot(t_mid, ones) + _dot(t_lo, ones)) * (1.0 / MOBA_BLOCK)
        km = jnp.where(lane == step * (PAGES_PER_STEP // pages_per_block) + blk, mean, km)
    km_ref[...] = km

    @pl.when(step == pl.num_programs(1) - 1)
    def _():
        q = q_ref[0]
        rows = q.shape[0]
        km_hi, km_lo = _split(km)
        gate = _dot(q, km_hi) + _dot(q, km_lo)
        blk_lane = lax.broadcasted_iota(jnp.int32, (rows, LANES), 1)
        lane_f = blk_lane.astype(F32)
        gate = jnp.where(blk_lane < n_blocks, gate, -jnp.inf)
        chosen = jnp.zeros((rows, LANES), F32)
        for _k in range(min(MOBA_TOPK, n_blocks)):
            best = jnp.max(gate, axis=1, keepdims=True)
            first = jnp.min(jnp.where(gate == best, lane_f, float(LANES)), axis=1, keepdims=True)
            pick = lane_f == first
            chosen = jnp.where(pick, 1.0, chosen)
            gate = jnp.where(pick, -jnp.inf, gate)
        o_ref[0] = chosen


def _moba_select(q_all, cache_kt, page_table):
    b, n_pages = page_table.shape
    w, page = cache_kt.shape[1:]
    rows = q_all.shape[1]
    steps = n_pages // PAGES_PER_STEP
    n_blocks = n_pages * page // MOBA_BLOCK
    assert n_blocks <= LANES
    grid_spec = pltpu.PrefetchScalarGridSpec(
        num_scalar_prefetch=1,
        grid=(b, steps),
        in_specs=[pl.BlockSpec((1, rows, w), lambda bi, s, pt: (bi, 0, 0))] + _page_specs(n_pages, page, w),
        out_specs=pl.BlockSpec((1, rows, LANES), lambda bi, s, pt: (bi, 0, 0)),
        scratch_shapes=[pltpu.VMEM((w, LANES), F32)],
    )
    return pl.pallas_call(
        functools.partial(_moba_select_body, n_blocks=n_blocks),
        out_shape=jax.ShapeDtypeStruct((b, rows, LANES), F32),
        grid_spec=grid_spec,
        compiler_params=_params(("arbitrary", "arbitrary")),
        name="moba_select",
    )(page_table, q_all, *([cache_kt] * PAGES_PER_STEP))


def _moba_sample_body(pt_ref, q_ref, kn_ref, vn_ref, sel_ref, hm_ref, *rest, t_new):
    k_refs = rest[:PAGES_PER_STEP]
    v_refs = rest[PAGES_PER_STEP:2 * PAGES_PER_STEP]
    o_ref = rest[2 * PAGES_PER_STEP]
    m_ref, l_ref, acc_ref = rest[2 * PAGES_PER_STEP + 1:]
    step = pl.program_id(1)
    q = q_ref[0]
    rows = q.shape[0]
    page = k_refs[0].shape[1]
    lane = lax.broadcasted_iota(jnp.int32, (rows, LANES), 1)

    @pl.when(step == 0)
    def _():
        kn = kn_ref[0].astype(BF16)
        vn = vn_ref[0].astype(BF16)
        qt = lax.broadcasted_iota(jnp.int32, (rows, LANES), 0) % t_new
        ok = lane <= qt
        s = jnp.where(ok, _dot_nt(q, kn), NEG_BIG)
        m = jnp.max(s, axis=1, keepdims=True)
        p = jnp.where(ok, jnp.exp(s - m), 0.0)
        m_ref[...] = jnp.broadcast_to(m, (rows, LANES))
        l_ref[...] = jnp.broadcast_to(jnp.sum(p, axis=1, keepdims=True), (rows, LANES))
        acc_ref[...] = _dot(p.astype(BF16), vn)

    chosen = sel_ref[0]
    pages_per_block = MOBA_BLOCK // page
    scores = []
    for i in range(PAGES_PER_STEP):
        blk = (step * PAGES_PER_STEP + i) // pages_per_block
        picked = jnp.sum(jnp.where(lane == blk, chosen, 0.0), axis=1, keepdims=True)
        ok = jnp.broadcast_to(picked, (rows, page)) > 0.5
        scores.append(jnp.where(ok, _dot(q, k_refs[i][...].astype(BF16)), NEG_BIG))
    m = m_ref[:, 0:1]
    m_new = m
    for s in scores:
        m_new = jnp.maximum(m_new, jnp.max(s, axis=1, keepdims=True))
    alpha = jnp.exp(m - m_new)
    l = alpha * l_ref[:, 0:1]
    acc = alpha * acc_ref[...]
    for i, s in enumerate(scores):
        p = jnp.where(s > 0.5 * NEG_BIG, jnp.exp(s - m_new), 0.0)
        l = l + jnp.sum(p, axis=1, keepdims=True)
        acc = acc + _dot_nt(p.astype(BF16), v_refs[i][...].astype(BF16))
    m_ref[...] = jnp.broadcast_to(m_new, (rows, LANES))
    l_ref[...] = jnp.broadcast_to(l, (rows, LANES))
    acc_ref[...] = acc

    @pl.when(step == pl.num_programs(1) - 1)
    def _():
        o_ref[0] = (acc / l) * hm_ref[...]


def _moba_sample(q_all, head_mask, k_new, v_new, chosen, cache_kt, cache_vt, page_table, t_new):
    b, rows, w = q_all.shape
    n_pages = page_table.shape[1]
    page = cache_kt.shape[2]
    assert page == LANES
    kn = _pad_rows(k_new, LANES)
    vn = _pad_rows(v_new, LANES)
    specs = _page_specs(n_pages, page, w)
    per_b = lambda shp: pl.BlockSpec((1,) + shp, lambda bi, s, pt: (bi, 0, 0))
    const = lambda shp: pl.BlockSpec(shp, lambda bi, s, pt: (0, 0))
    grid_spec = pltpu.PrefetchScalarGridSpec(
        num_scalar_prefetch=1,
        grid=(b, n_pages // PAGES_PER_STEP),
        in_specs=[per_b((rows, w)), per_b((LANES, w)), per_b((LANES, w)), per_b((rows, LANES)),
                  const((rows, w))] + specs + specs,
        out_specs=per_b((rows, w)),
        scratch_shapes=[pltpu.VMEM((rows, LANES), F32), pltpu.VMEM((rows, LANES), F32),
                        pltpu.VMEM((rows, w), F32)],
    )
    o = pl.pallas_call(
        functools.partial(_moba_sample_body, t_new=t_new),
        out_shape=jax.ShapeDtypeStruct((b, rows, w), F32),
        grid_spec=grid_spec,
        compiler_params=_params(("arbitrary", "arbitrary")),
        name="moba_sample",
    )(page_table, q_all, kn, vn, chosen, head_mask, *([cache_kt] * PAGES_PER_STEP),
      *([cache_vt] * PAGES_PER_STEP))
    return _collapse_heads(o, t_new)


def _merge_body(osb_ref, omb_ref, sgsb_ref, sgmb_ref, x_ref, ga_ref, sh_ref, sc_ref, g2_ref, wsb_ref,
                wmb_ref, wout_ref, wr_ref, br_ref, x1_ref, h2_ref, eid_ref, gate_ref):
    u = (sgsb_ref[0].astype(F32) * _dot(osb_ref[0], wsb_ref[...])
         + sgmb_ref[0].astype(F32) * _dot(omb_ref[0], wmb_ref[...]))
    x1 = x_ref[0] + ga_ref[0] * _dot(u.astype(BF16), wout_ref[...])
    x1_ref[0] = x1
    h2 = _modulate(x1, g2_ref[...], sh_ref[0], sc_ref[0])
    h2_ref[0] = h2
    logits = _dot3(h2, wr_ref[...]) + br_ref[...]
    rows = logits.shape[0]
    lane = lax.broadcasted_iota(jnp.int32, (rows, LANES), 1)
    lane_f = lane.astype(F32)
    eid = jnp.zeros((rows, LANES), F32)
    top = jnp.zeros((rows, LANES), F32)
    best0 = None
    for k in range(TOP_K):
        best = jnp.max(logits, axis=1, keepdims=True)
        first = jnp.min(jnp.where(logits == best, lane_f, float(LANES)), axis=1, keepdims=True)
        logits = jnp.where(lane_f == first, -jnp.inf, logits)
        if best0 is None:
            best0 = best
        eid = jnp.where(lane == k, first, eid)
        top = jnp.where(lane == k, jnp.exp(best - best0), top)
    eid_ref[0] = eid.astype(jnp.int32)
    gate_ref[0] = top / jnp.sum(top, axis=1, keepdims=True)


def _merge(o_sb, o_mb, sg_sb, sg_mb, x, ga1, sh2, sc2, g2, w_sb16, w_mb16, w_out16, w_router_pad,
           b_router_pad, tm):
    b, t, d = x.shape
    assert d == SUBLANES * LANES
    r = ga1.shape[1]
    nt = t // tm
    tile = lambda wd: pl.BlockSpec((1, tm, wd), lambda bi, ti: (bi, ti, 0))
    const = lambda shp: pl.BlockSpec(shp, lambda bi, ti: (0,) * len(shp))
    mod = pl.BlockSpec((1, r, d), lambda bi, ti: (bi, 0, 0))
    return pl.pallas_call(
        _merge_body,
        out_shape=[jax.ShapeDtypeStruct((b, t, d), F32), jax.ShapeDtypeStruct((b, t, d), F32),
                   jax.ShapeDtypeStruct((b, t, LANES), jnp.int32), jax.ShapeDtypeStruct((b, t, LANES), F32)],
        grid=(b, nt),
        in_specs=[tile(WIDTH), tile(WIDTH), tile(d), tile(d), tile(d), mod, mod, mod, const((1, d)),
                  const((WIDTH, d)), const((WIDTH, d)), const((d, d)), const((d, LANES)), const((1, LANES))],
        out_specs=[tile(d), tile(d), tile(LANES), tile(LANES)],
        compiler_params=_params(("arbitrary", "arbitrary")),
        name="merge",
    )(o_sb, o_mb, sg_sb, sg_mb, x, ga1, sh2, sc2, g2, w_sb16, w_mb16, w_out16, w_router_pad, b_router_pad)


def _ffn_body(be_ref, nu_ref, tok_ref, tokn_ref, dst_ref, dstp_ref, h2_hbm, wgu_ref, bgu_ref, wd_ref, bd_ref,
              y_hbm, xbuf, obuf, wgu16, wd16, gsem, ssem, *, tm, ff):
    j = pl.program_id(0)
    n_used = nu_ref[0]
    cur = j % 2
    groups = tm // SUBLANES

    def gather_row(idx_ref, grp, sub, buf):
        tok = idx_ref[0, 0, grp * SUBLANES + sub]
        return pltpu.make_async_copy(h2_hbm.at[tok >> 3, :, tok & 7, :], xbuf.at[buf, grp, :, sub, :],
                                     gsem.at[buf])

    def scatter_row(idx_ref, grp, sub, buf):
        dst = idx_ref[0, 0, grp * SUBLANES + sub]
        return pltpu.make_async_copy(obuf.at[buf, grp, :, sub, :], y_hbm.at[dst >> 3, :, dst & 7, :],
                                     ssem.at[0])

    def for_each_group(fn):
        def body(grp, c):
            for sub in range(SUBLANES):
                fn(grp, sub)
            return c
        lax.fori_loop(0, groups, body, 0)

    def wait_gather(buf):
        pltpu.make_async_copy(h2_hbm.at[pl.ds(0, groups)], xbuf.at[buf], gsem.at[buf]).wait()

    def wait_scatter(buf):
        pltpu.make_async_copy(obuf.at[buf], y_hbm.at[pl.ds(0, groups)], ssem.at[0]).wait()

    @pl.when(j == 0)
    def _():
        n_slot_groups = y_hbm.shape[0] - 2 * groups
        obuf[...] = jnp.zeros(obuf.shape, F32)
        for half in range(2):
            fill = pltpu.make_async_copy(obuf.at[0], y_hbm.at[pl.ds(n_slot_groups + half * groups, groups)],
                                         ssem.at[0])
            fill.start()
            fill.wait()
        for_each_group(lambda grp, sub: gather_row(tok_ref, grp, sub, 0).start())

    @pl.when(j < n_used)
    def _():
        @pl.when(j >= 1)
        def _():
            wait_scatter(cur)

        @pl.when((j == 0) | (be_ref[j] != be_ref[jnp.maximum(j - 1, 0)]))
        def _():
            wgu16[...] = wgu_ref[0].astype(BF16)
            wd16[...] = wd_ref[0].astype(BF16)

        wait_gather(cur)
        x16 = jnp.concatenate([xbuf[cur, :, c, :, :].reshape(tm, LANES).astype(BF16)
                               for c in range(SUBLANES)], axis=1)
        y = jnp.zeros((tm, wd16.shape[1]), F32) + bd_ref[0]
        chunk = 512
        n_chunks = ff // chunk
        issue_until = [min(tm, -(-tm // (n_chunks - 1)) * (c + 1)) for c in range(n_chunks - 1)] + [tm]
        for c in range(n_chunks):
            g = _dot(x16, wgu16[:, c * chunk:(c + 1) * chunk]) + bgu_ref[0, :, c * chunk:(c + 1) * chunk]
            u = (_dot(x16, wgu16[:, ff + c * chunk:ff + (c + 1) * chunk])
                 + bgu_ref[0, :, ff + c * chunk:ff + (c + 1) * chunk])
            g = jnp.minimum(g, SWIGLU_LIMIT)
            u = jnp.clip(u, -SWIGLU_LIMIT, SWIGLU_LIMIT)
            act = (u + 1.0) * (g * jax.nn.sigmoid(SWIGLU_ALPHA * g))
            y = y + _dot(act.astype(BF16), wd16[c * chunk:(c + 1) * chunk, :])
            for r in range(issue_until[c - 1] if c else 0, issue_until[c]):
                gather_row(tokn_ref, r // SUBLANES, r % SUBLANES, 1 - cur).start(priority=0)
                scatter_row(dstp_ref, r // SUBLANES, r % SUBLANES, 1 - cur).start(priority=1)

        for c in range(SUBLANES):
            obuf[cur, :, c, :, :] = y[:, c * LANES:(c + 1) * LANES].reshape(groups, SUBLANES, LANES)

        @pl.when(j == n_used - 1)
        def _():
            wait_scatter(1 - cur)
            wait_gather(1 - cur)
            for_each_group(lambda grp, sub: scatter_row(dst_ref, grp, sub, cur).start())
            wait_scatter(cur)


def _row_groups(x):
    rows, d = x.shape
    return x.reshape(rows // SUBLANES, SUBLANES, d // LANES, LANES).transpose(0, 2, 1, 3)


def _expert_ffn(h2_rows, row_tok, row_dst, row_dst_prev, block_e, n_used, w_gu, b_gu, w_down, b_down,
                n_out_rows):
    n_blocks, _, tm = row_tok.shape
    e, d, ff2 = w_gu.shape
    ff = ff2 // 2
    assert d == SUBLANES * LANES
    groups = tm // SUBLANES

    def blk(j, be, nu):
        return (jnp.minimum(j, nu[0] - 1), 0, 0)

    def blk_next(j, be, nu):
        return (jnp.minimum(j + 1, nu[0] - 1), 0, 0)

    smem = lambda index: pl.BlockSpec((1, 1, tm), index, memory_space=pltpu.SMEM)
    per_e = lambda shp: pl.BlockSpec((1,) + shp, lambda j, be, nu: (be[j], 0, 0))
    grid_spec = pltpu.PrefetchScalarGridSpec(
        num_scalar_prefetch=2,
        grid=(n_blocks,),
        in_specs=[smem(blk), smem(blk_next), smem(blk), smem(blk), pl.BlockSpec(memory_space=pl.ANY),
                  per_e((d, ff2)), per_e((1, ff2)), per_e((ff, d)), per_e((1, d))],
        out_specs=pl.BlockSpec(memory_space=pl.ANY),
        scratch_shapes=[pltpu.VMEM((2, groups, SUBLANES, SUBLANES, LANES), F32),
                        pltpu.VMEM((2, groups, SUBLANES, SUBLANES, LANES), F32),
                        pltpu.VMEM((d, ff2), BF16), pltpu.VMEM((ff, d), BF16),
                        pltpu.SemaphoreType.DMA((2,)), pltpu.SemaphoreType.DMA((1,))],
    )
    y = pl.pallas_call(
        functools.partial(_ffn_body, tm=tm, ff=ff),
        out_shape=jax.ShapeDtypeStruct((n_out_rows // SUBLANES, SUBLANES, SUBLANES, LANES), F32),
        grid_spec=grid_spec,
        compiler_params=_params(("arbitrary",), vmem=56 * 1024 * 1024),
        name="expert_ffn",
    )(block_e, n_used, row_tok, row_tok, row_dst, row_dst_prev, _row_groups(h2_rows), w_gu,
      b_gu.reshape(e, 1, ff2), w_down, b_down.reshape(e, 1, d))
    return y.transpose(0, 2, 1, 3).reshape(n_out_rows, d)


def _route(eid, n_tokens):
    tile = EXPERT_TILE
    nk = n_tokens * TOP_K
    n_blocks = (nk + N_EXPERTS * (tile - 1) + tile - 1) // tile
    rows = n_blocks * tile
    onehot = (eid[:, :, None] == jnp.arange(N_EXPERTS, dtype=jnp.int32)[None, None, :]).astype(jnp.int32)
    per_token = onehot.sum(axis=1)
    before = jnp.cumsum(per_token, axis=0) - per_token
    counts = per_token.sum(axis=0)
    padded = (counts + tile - 1) // tile * tile
    pend = jnp.cumsum(padded)
    pstart = pend - padded
    dest = ((before + pstart[None, :])[:, None, :] * onehot).sum(axis=2).reshape(-1).astype(jnp.int32)
    pos = jnp.arange(rows, dtype=jnp.int32)
    row_flat = jnp.full((rows,), -1, jnp.int32).at[dest].set(
        jnp.arange(nk, dtype=jnp.int32), unique_indices=True)
    is_pad = row_flat < 0
    dump = nk + ((pos // tile) % 2) * tile + pos % tile
    row_tok = jnp.where(is_pad, 0, row_flat // TOP_K).astype(jnp.int32)
    row_dst = jnp.where(is_pad, dump, (row_flat % TOP_K) * n_tokens + row_flat // TOP_K).astype(jnp.int32)
    n_used = (pend[-1] // tile).astype(jnp.int32)
    blk_start = jnp.minimum(jnp.arange(n_blocks), n_used - 1) * tile
    block_e = jnp.minimum((pend[None, :] <= blk_start[:, None]).sum(axis=1), N_EXPERTS - 1).astype(jnp.int32)
    row_dst = row_dst.reshape(n_blocks, 1, tile)
    first = (nk + tile + jnp.arange(tile, dtype=jnp.int32)).reshape(1, 1, tile)
    row_dst_prev = jnp.concatenate([first, row_dst[:-1]], axis=0)
    return (row_tok.reshape(n_blocks, 1, tile), row_dst, row_dst_prev, block_e, n_used.reshape(1),
            nk + 2 * tile)


def _combine_body(y0_ref, y1_ref, y2_ref, y3_ref, gate_ref, x1_ref, ga_ref, o_ref):
    gate = gate_ref[0]
    ffn = None
    for k, y_ref in enumerate((y0_ref, y1_ref, y2_ref, y3_ref)):
        part = gate[:, k:k + 1] * y_ref[...]
        ffn = part if ffn is None else ffn + part
    o_ref[0] = x1_ref[0] + ga_ref[0] * ffn


def _combine(y, gate, x1, ga2, tm, first_token, n_tokens):
    b, t, d = x1.shape
    r = ga2.shape[1]
    nt = t // tm
    assert first_token % tm == 0 and n_tokens % tm == 0
    tile = lambda wd: pl.BlockSpec((1, tm, wd), lambda bi, ti: (bi, ti, 0))

    def y_spec(k):
        base = (k * n_tokens + first_token) // tm
        return pl.BlockSpec((tm, d), lambda bi, ti: (base + bi * nt + ti, 0))

    return pl.pallas_call(
        _combine_body,
        out_shape=jax.ShapeDtypeStruct((b, t, d), F32),
        grid=(b, nt),
        in_specs=[y_spec(k) for k in range(TOP_K)]
        + [tile(LANES), tile(d), pl.BlockSpec((1, r if r == 1 else tm, d),
                                              lambda bi, ti: (bi, 0, 0) if r == 1 else (bi, ti, 0))],
        out_specs=tile(d),
        compiler_params=_params(("arbitrary", "arbitrary")),
        name="combine",
    )(y, y, y, y, gate, x1, ga2)


def _head_gain(g):
    return jnp.tile(g.astype(F32), N_HEADS).reshape(1, WIDTH)


def _mods(mod, d):
    return [mod[:, None, i * d:(i + 1) * d] for i in range(N_MOD)]


def kernel(x_prompt, x_sample, c_prompt, c_sample, cache_sb_k, cache_sb_v, cache_mb_k, cache_mb_v, page_table,
           w_ada, b_ada, g_norm1, g_norm2, w_in, q_norm_g, k_norm_g, w_br_sb, w_br_mb, w_out, w_router,
           b_router, w_gu, b_gu, w_down, b_down):
    depth = w_ada.shape[0]
    assert depth == 1
    bp, seq, d = x_prompt.shape
    bs, t_new, _ = x_sample.shape
    page = cache_sb_k.shape[2]
    past = page_table.shape[1] * page
    n_prompt = bp * seq
    n_sample = bs * t_new
    lyr = 0

    mod = _adaln(jnp.concatenate([c_prompt, c_sample], axis=0), w_ada[lyr], b_ada[lyr])
    sh1p, sc1p, ga1p, sh2p, sc2p, ga2p = _mods(mod[:bp], d)
    per_row = lambda m: jnp.repeat(m, t_new, axis=0).reshape(1, n_sample, d)
    sh1s, sc1s, ga1s, sh2s, sc2s, ga2s = [per_row(m[:, 0]) for m in _mods(mod[bp:], d)]

    w_in16 = w_in[lyr].astype(BF16)
    g1 = g_norm1[lyr].reshape(1, d)
    g2 = g_norm2[lyr].reshape(1, d)
    qg = _head_gain(q_norm_g[lyr])
    kg = _head_gain(k_norm_g[lyr])
    head_of = np.arange(WIDTH) // HEAD_DIM
    gmat = jnp.asarray((head_of[:, None] == head_of[None, :]).astype(np.float32) / HEAD_DIM, dtype=BF16)

    pos_p = jnp.arange(seq, dtype=jnp.int32)
    (qsb, ksb, vsb, ksb16, vsb16, qmb, kmb, vmb, kmb16, _, sgsb, sgmb, kmean, vmbt16) = _inproj(
        x_prompt, sh1p, sc1p, g1, w_in16, pos_p, qg, kg, gmat, ROW_TILE, True)
    o_sb_p = _sb_prompt(qsb, ksb16, vsb16)
    o_mb_p = _moba_prompt(qmb, kmb16, vmbt16, kmean)

    xs = x_sample.reshape(1, n_sample, d)
    pos_s = past + jnp.arange(n_sample, dtype=jnp.int32) % t_new
    (qsb_s, ksb_s, vsb_s, _, _, qmb_s, kmb_s, vmb_s, _, _, sgsb_s, sgmb_s) = _inproj(
        xs, sh1s, sc1s, g1, w_in16, pos_s, qg, kg, gmat, n_sample, False)
    by_batch = lambda a: a.reshape(bs, t_new, WIDTH)
    o_sb_s = _sb_sample(by_batch(qsb_s), by_batch(ksb_s), by_batch(vsb_s), _pages_t(cache_sb_k, lyr),
                        _pages_t(cache_sb_v, lyr), page_table)
    mb_kt = _pages_t(cache_mb_k, lyr)
    q_all_mb, head_mask = _expand_heads(by_batch(qmb_s))
    chosen = _moba_select(q_all_mb, mb_kt, page_table)
    o_mb_s = _moba_sample(q_all_mb, head_mask, by_batch(kmb_s), by_batch(vmb_s), chosen, mb_kt,
                          _pages_t(cache_mb_v, lyr), page_table, t_new)
    o_sb_s = o_sb_s.reshape(1, n_sample, WIDTH).astype(BF16)
    o_mb_s = o_mb_s.reshape(1, n_sample, WIDTH).astype(BF16)

    w_sb16 = w_br_sb[lyr].astype(BF16)
    w_mb16 = w_br_mb[lyr].astype(BF16)
    w_out16 = w_out[lyr].astype(BF16)
    w_router_pad = jnp.pad(w_router[lyr], ((0, 0), (0, LANES - N_EXPERTS)))
    b_router_pad = jnp.concatenate([b_router[lyr].astype(F32),
                                    jnp.full((LANES - N_EXPERTS,), -jnp.inf, F32)]).reshape(1, LANES)
    x1p, h2p, eidp, gatep = _merge(o_sb_p, o_mb_p, sgsb, sgmb, x_prompt, ga1p, sh2p, sc2p, g2, w_sb16,
                                   w_mb16, w_out16, w_router_pad, b_router_pad, ROW_TILE)
    x1s, h2s, eids, gates = _merge(o_sb_s, o_mb_s, sgsb_s, sgmb_s, xs, ga1s, sh2s, sc2s, g2, w_sb16,
                                   w_mb16, w_out16, w_router_pad, b_router_pad, n_sample)

    n_tokens = n_prompt + n_sample
    h2_rows = jnp.concatenate([h2p.reshape(n_prompt, d), h2s.reshape(n_sample, d)])
    eid = jnp.concatenate([eidp.reshape(n_prompt, LANES), eids.reshape(n_sample, LANES)])[:, :TOP_K]
    row_tok, row_dst, row_dst_prev, block_e, n_used, n_out_rows = _route(eid, n_tokens)
    y_rows = _expert_ffn(h2_rows, row_tok, row_dst, row_dst_prev, block_e, n_used, w_gu[lyr], b_gu[lyr],
                         w_down[lyr], b_down[lyr], n_out_rows)
    y_prompt = _combine(y_rows, gatep, x1p, ga2p, COMBINE_TILE, 0, n_tokens)
    y_sample = _combine(y_rows, gates, x1s, ga2s, COMBINE_TILE, n_prompt, n_tokens)

    heads = lambda a, b_, t_: a.reshape(1, b_, t_, N_HEADS, HEAD_DIM)
    return (y_prompt, y_sample.reshape(bs, t_new, d),
            heads(ksb, bp, seq), heads(vsb, bp, seq), heads(kmb, bp, seq), heads(vmb, bp, seq),
            heads(ksb_s, bs, t_new), heads(vsb_s, bs, t_new), heads(kmb_s, bs, t_new), heads(vmb_s, bs, t_new))
```
